```python
import math, functools
import jax, jax.numpy as jnp
from jax import lax
import numpy as np

D_MODEL = 1024
BATCH = 2
SEQ = 8192
DEPTH = 1
DEC_BATCH = 128
DEC_SEQ = 8
PAST_LEN = 2048
PAGE_SIZE = 128

D_RNN = D_MODEL
N_LRU_BLOCKS = 8
LRU_BLOCK = D_RNN // N_LRU_BLOCKS
LRU_CONV_W = 4
LRU_C = 8.0
N_HEADS = 8
HEAD_DIM = D_MODEL // (2 * N_HEADS)
V_DIM = 2 * HEAD_DIM
QK_WIDTH = N_HEADS * 2 * HEAD_DIM
V_WIDTH = N_HEADS * V_DIM
Q_BLOCK = 128
N_MEM = 256
MEM_HEADS = 4
MEM_HEAD_DIM = D_MODEL // MEM_HEADS
MEM_WIDTH = MEM_HEADS * MEM_HEAD_DIM
N_BRANCH = 3
D_FF = 3 * D_MODEL
FFN_CONV_W = 3
RMS_EPS = 1e-6
NEG_INF = -1e30
IN_SPLITS = (D_RNN, D_RNN, QK_WIDTH, QK_WIDTH, V_WIDTH, MEM_WIDTH, N_BRANCH * D_MODEL)
IN_WIDTH = sum(IN_SPLITS)

kernel_name = "hawk_diffattn_memxattn_convffn_step"


def rms_norm(x, g):
    xf = x.astype(jnp.float32)
    y = xf * lax.rsqrt(jnp.mean(xf * xf, axis=-1, keepdims=True) + RMS_EPS)
    return y.astype(x.dtype) * g


def causal_dwconv(x, prev, w, b):
    width = w.shape[0]
    t = x.shape[1]
    xp = jnp.concatenate([prev.astype(x.dtype), x], axis=1)
    y = sum(xp[:, j:j + t] * w[j] for j in range(width)) + b
    return y, xp[:, xp.shape[1] - (width - 1):]


def rg_lru(x, h0, w_rg_a, b_rg_a, w_rg_i, b_rg_i, lru_lambda):
    b, t, c = x.shape
    xb = x.reshape(b, t, N_LRU_BLOCKS, LRU_BLOCK)
    r = jax.nn.sigmoid((jnp.einsum('btnc,ncd->btnd', xb, w_rg_a).reshape(b, t, c) + b_rg_a).astype(jnp.float32))
    i = jax.nn.sigmoid((jnp.einsum('btnc,ncd->btnd', xb, w_rg_i).reshape(b, t, c) + b_rg_i).astype(jnp.float32))
    log_a = -LRU_C * jax.nn.softplus(-lru_lambda.astype(jnp.float32)) * r
    a = jnp.exp(log_a)
    u = jnp.sqrt(-jnp.expm1(2.0 * log_a)) * (i * x.astype(jnp.float32))

    def step(h, au):
        a_t, u_t = au
        h = a_t * h + u_t
        return h, h

    h_last, hs = lax.scan(step, h0.astype(jnp.float32), (a.transpose(1, 0, 2), u.transpose(1, 0, 2)))
    return hs.transpose(1, 0, 2).astype(x.dtype), h_last


def diff_attn_prompt(q, k, v, lam):
    b, t = q.shape[:2]
    nb = t // Q_BLOCK
    scale = 1.0 / math.sqrt(HEAD_DIM)
    kpos = jnp.arange(t)
    qb = q.reshape(b, nb, Q_BLOCK, N_HEADS, 2, HEAD_DIM).transpose(1, 0, 2, 3, 4, 5)

    def block(args):
        qi, idx = args
        s = jnp.einsum('bqhcd,bkhcd->bhcqk', qi, k).astype(jnp.float32) * scale
        qpos = idx * Q_BLOCK + jnp.arange(Q_BLOCK)
        s = jnp.where(kpos[None, :] <= qpos[:, None], s, NEG_INF)
        p = jax.nn.softmax(s, axis=-1)
        a = p[:, :, 0] - lam * p[:, :, 1]
        return jnp.einsum('bhqk,bkhe->bqhe', a.astype(v.dtype), v)

    o = lax.map(block, (qb, jnp.arange(nb)))
    return o.transpose(1, 0, 2, 3, 4).reshape(b, t, N_HEADS, V_DIM)


def diff_attn_sample(q, k, v, lam, cache_k, cache_v, page_table):
    bd, t = q.shape[:2]
    scale = 1.0 / math.sqrt(HEAD_DIM)
    kp = cache_k[page_table].reshape(bd, -1, N_HEADS, 2, HEAD_DIM).astype(q.dtype)
    vp = cache_v[page_table].reshape(bd, -1, N_HEADS, V_DIM).astype(v.dtype)
    n_past = kp.shape[1]
    s_past = jnp.einsum('bqhcd,bkhcd->bhcqk', q, kp).astype(jnp.float32) * scale
    s_new = jnp.einsum('bqhcd,bkhcd->bhcqk', q, k).astype(jnp.float32) * scale
    causal = jnp.tril(jnp.ones((t, t), dtype=bool))
    s_new = jnp.where(causal, s_new, NEG_INF)
    p = jax.nn.softmax(jnp.concatenate([s_past, s_new], axis=-1), axis=-1)
    a = (p[:, :, 0] - lam * p[:, :, 1]).astype(v.dtype)
    return (jnp.einsum('bhqk,bkhe->bqhe', a[..., :n_past], vp)
            + jnp.einsum('bhqk,bkhe->bqhe', a[..., n_past:], v))


def memory_kv(mem, g_mem, w_mem_kv):
    b, n, _ = mem.shape
    kv = rms_norm(mem, g_mem) @ w_mem_kv
    mk, mv = jnp.split(kv, 2, axis=-1)
    return mk.reshape(b, n, MEM_HEADS, MEM_HEAD_DIM), mv.reshape(b, n, MEM_HEADS, MEM_HEAD_DIM)


def memory_attn(cq, mk, mv):
    b, t, _ = cq.shape
    q = cq.reshape(b, t, MEM_HEADS, MEM_HEAD_DIM)
    s = jnp.einsum('bqhd,bkhd->bhqk', q, mk.astype(q.dtype)).astype(jnp.float32) / math.sqrt(MEM_HEAD_DIM)
    p = jax.nn.softmax(s, axis=-1).astype(cq.dtype)
    return jnp.einsum('bhqk,bkhd->bqhd', p, mv.astype(cq.dtype)).reshape(b, t, MEM_WIDTH)


def lambda_init_fn(layer_idx):
    return 0.8 - 0.6 * math.exp(-0.3 * layer_idx)


def layer(x, attn_fn, mem_k, mem_v, lru_h0, lru_conv0, ffn_conv0, lam, lam_init,
          g_pre_mix, w_in, w_lru_conv, b_lru_conv, w_rg_a, b_rg_a, w_rg_i, b_rg_i, lru_lambda,
          g_subln, w_br_lru, w_br_attn, w_br_mem, w_out, g_post_mix,
          g_pre_ffn, w_up, w_ffn_conv, b_ffn_conv, w_down, g_post_ffn):
    b, t, _ = x.shape
    h = rms_norm(x, g_pre_mix)
    z = h @ w_in
    offsets = np.cumsum(IN_SPLITS)[:-1].tolist()
    lru_x, lru_y, q, k, v, cq, gates = jnp.split(z, offsets, axis=-1)
    xc, lru_conv_new = causal_dwconv(lru_x, lru_conv0, w_lru_conv, b_lru_conv)
    hs, lru_h_new = rg_lru(xc, lru_h0, w_rg_a, b_rg_a, w_rg_i, b_rg_i, lru_lambda)
    br_lru = jax.nn.gelu(lru_y) * hs
    q = q.reshape(b, t, N_HEADS, 2, HEAD_DIM)
    k = k.reshape(b, t, N_HEADS, 2, HEAD_DIM)
    v = v.reshape(b, t, N_HEADS, V_DIM)
    o = attn_fn(q, k, v, lam)
    br_attn = (rms_norm(o, g_subln) * (1.0 - lam_init)).reshape(b, t, V_WIDTH)
    br_mem = memory_attn(cq, mem_k, mem_v)
    g = jax.nn.sigmoid(gates.astype(jnp.float32)).astype(x.dtype).reshape(b, t, N_BRANCH, D_MODEL)
    m = (g[:, :, 0] * (br_lru @ w_br_lru) + g[:, :, 1] * (br_attn @ w_br_attn)
         + g[:, :, 2] * (br_mem @ w_br_mem))
    x = x + rms_norm(m @ w_out, g_post_mix)
    up, ffn_conv_new = causal_dwconv(rms_norm(x, g_pre_ffn) @ w_up, ffn_conv0, w_ffn_conv, b_ffn_conv)
    gate, val = jnp.split(up, 2, axis=-1)
    x = x + rms_norm((jax.nn.gelu(gate) * val) @ w_down, g_post_ffn)
    return x, k, v, lru_h_new, lru_conv_new, ffn_conv_new


def setup_inputs(seed: int = 0) -> dict:
    key = jax.random.key(seed)
    ks = iter(jax.random.split(key, 64))
    f32 = jnp.float32

    def nrm(shape, s):
        return jax.random.normal(next(ks), shape, f32) * s

    def gain(shape):
        return 1.0 + nrm(shape, 0.05)

    n_pages = PAST_LEN // PAGE_SIZE
    n_used = DEC_BATCH * n_pages
    n_pool = (5 * n_used + 3) // 4
    page_table = jax.random.permutation(next(ks), n_pool)[:n_used].reshape(DEC_BATCH, n_pages).astype(jnp.int32)
    a0 = jax.random.uniform(next(ks), (DEPTH, D_RNN), f32, 0.9, 0.999)
    return {
        "x_prompt": nrm((BATCH, SEQ, D_MODEL), 1.0),
        "x_sample": nrm((DEC_BATCH, DEC_SEQ, D_MODEL), 1.0),
        "mem_prompt": nrm((BATCH, N_MEM, D_MODEL), 1.0),
        "cache_k": nrm((DEPTH, n_pool, PAGE_SIZE, N_HEADS, 2, HEAD_DIM), 1.0),
        "cache_v": nrm((DEPTH, n_pool, PAGE_SIZE, N_HEADS, V_DIM), 1.0),
        "page_table": page_table,
        "cache_mem_k": nrm((DEPTH, DEC_BATCH, N_MEM, MEM_HEADS, MEM_HEAD_DIM), 1.0),
        "cache_mem_v": nrm((DEPTH, DEC_BATCH, N_MEM, MEM_HEADS, MEM_HEAD_DIM), 1.0),
        "state_lru_h": nrm((DEPTH, DEC_BATCH, D_RNN), 0.5),
        "state_lru_conv": nrm((DEPTH, DEC_BATCH, LRU_CONV_W - 1, D_RNN), 1.0),
        "state_ffn_conv": nrm((DEPTH, DEC_BATCH, FFN_CONV_W - 1, 2 * D_FF), 1.0),
        "g_pre_mix": gain((DEPTH, D_MODEL)),
        "w_in": nrm((DEPTH, D_MODEL, IN_WIDTH), D_MODEL ** -0.5),
        "w_lru_conv": nrm((DEPTH, LRU_CONV_W, D_RNN), LRU_CONV_W ** -0.5),
        "b_lru_conv": nrm((DEPTH, D_RNN), 0.01),
        "w_rg_a": nrm((DEPTH, N_LRU_BLOCKS, LRU_BLOCK, LRU_BLOCK), LRU_BLOCK ** -0.5),
        "b_rg_a": nrm((DEPTH, D_RNN), 0.01),
        "w_rg_i": nrm((DEPTH, N_LRU_BLOCKS, LRU_BLOCK, LRU_BLOCK), LRU_BLOCK ** -0.5),
        "b_rg_i": nrm((DEPTH, D_RNN), 0.01),
        "lru_lambda": jnp.log(a0) - jnp.log1p(-a0),
        "lambda_q1": nrm((DEPTH, HEAD_DIM), 0.1),
        "lambda_k1": nrm((DEPTH, HEAD_DIM), 0.1),
        "lambda_q2": nrm((DEPTH, HEAD_DIM), 0.1),
        "lambda_k2": nrm((DEPTH, HEAD_DIM), 0.1),
        "g_subln": gain((DEPTH, V_DIM)),
        "g_mem": gain((DEPTH, D_MODEL)),
        "w_mem_kv": nrm((DEPTH, D_MODEL, 2 * MEM_WIDTH), D_MODEL ** -0.5),
        "w_br_lru": nrm((DEPTH, D_RNN, D_MODEL), D_RNN ** -0.5),
        "w_br_attn": nrm((DEPTH, V_WIDTH, D_MODEL), V_WIDTH ** -0.5),
        "w_br_mem": nrm((DEPTH, MEM_WIDTH, D_MODEL), MEM_WIDTH ** -0.5),
        "w_out": nrm((DEPTH, D_MODEL, D_MODEL), D_MODEL ** -0.5),
        "g_post_mix": gain((DEPTH, D_MODEL)),
        "g_pre_ffn": gain((DEPTH, D_MODEL)),
        "w_up": nrm((DEPTH, D_MODEL, 2 * D_FF), D_MODEL ** -0.5),
        "w_ffn_conv": nrm((DEPTH, FFN_CONV_W, 2 * D_FF), FFN_CONV_W ** -0.5),
        "b_ffn_conv": nrm((DEPTH, 2 * D_FF), 0.01),
        "w_down": nrm((DEPTH, D_FF, D_MODEL), D_FF ** -0.5),
        "g_post_ffn": gain((DEPTH, D_MODEL)),
    }


def reference(x_prompt, x_sample, mem_prompt, cache_k, cache_v, page_table, cache_mem_k, cache_mem_v,
              state_lru_h, state_lru_conv, state_ffn_conv,
              g_pre_mix, w_in, w_lru_conv, b_lru_conv, w_rg_a, b_rg_a, w_rg_i, b_rg_i, lru_lambda,
              lambda_q1, lambda_k1, lambda_q2, lambda_k2, g_subln, g_mem, w_mem_kv,
              w_br_lru, w_br_attn, w_br_mem, w_out, g_post_mix,
              g_pre_ffn, w_up, w_ffn_conv, b_ffn_conv, w_down, g_post_ffn):
    xp, xs = x_prompt, x_sample
    bp = xp.shape[0]
    kp_l, vp_l, mkp_l, mvp_l, hp_l, cp_l, fp_l = [], [], [], [], [], [], []
    ks_l, vs_l, hs_l, cs_l, fs_l = [], [], [], [], []
    for l in range(DEPTH):
        lam_init = lambda_init_fn(l)
        lam = (jnp.exp(jnp.sum(lambda_q1[l] * lambda_k1[l]).astype(jnp.float32))
               - jnp.exp(jnp.sum(lambda_q2[l] * lambda_k2[l]).astype(jnp.float32)) + lam_init)
        lw = [a[l] for a in (g_pre_mix, w_in, w_lru_conv, b_lru_conv, w_rg_a, b_rg_a, w_rg_i, b_rg_i,
                             lru_lambda, g_subln, w_br_lru, w_br_attn, w_br_mem, w_out, g_post_mix,
                             g_pre_ffn, w_up, w_ffn_conv, b_ffn_conv, w_down, g_post_ffn)]
        mk_p, mv_p = memory_kv(mem_prompt, g_mem[l], w_mem_kv[l])
        xp, k_p, v_p, h_p, c_p, f_p = layer(
            xp, diff_attn_prompt, mk_p, mv_p,
            jnp.zeros((bp, D_RNN), jnp.float32),
            jnp.zeros((bp, LRU_CONV_W - 1, D_RNN), xp.dtype),
            jnp.zeros((bp, FFN_CONV_W - 1, 2 * D_FF), xp.dtype),
            lam, lam_init, *lw)
        attn_s = functools.partial(diff_attn_sample, cache_k=cache_k[l], cache_v=cache_v[l], page_table=page_table)
        xs, k_s, v_s, h_s, c_s, f_s = layer(
            xs, attn_s, cache_mem_k[l], cache_mem_v[l],
            state_lru_h[l], state_lru_conv[l], state_ffn_conv[l],
            lam, lam_init, *lw)
        kp_l.append(k_p); vp_l.append(v_p); mkp_l.append(mk_p); mvp_l.append(mv_p)
        hp_l.append(h_p); cp_l.append(c_p); fp_l.append(f_p)
        ks_l.append(k_s); vs_l.append(v_s); hs_l.append(h_s); cs_l.append(c_s); fs_l.append(f_s)
    return (xp, xs,
            jnp.stack(kp_l), jnp.stack(vp_l), jnp.stack(mkp_l), jnp.stack(mvp_l),
            jnp.stack(hp_l), jnp.stack(cp_l), jnp.stack(fp_l),
            jnp.stack(ks_l), jnp.stack(vs_l), jnp.stack(hs_l), jnp.stack(cs_l), jnp.stack(fs_l))
```

```python
import functools
import math

import jax
import jax.numpy as jnp
from jax import lax
from jax.experimental import pallas as pl
from jax.experimental.pallas import tpu as pltpu

F32 = jnp.float32
BF16 = jnp.bfloat16

D_MODEL = 1024
N_HEADS = 8
HEAD_DIM = 64
V_DIM = 128
N_MEM = 256
MEM_HEADS = 4
MEM_HEAD_DIM = 256
N_LRU_BLOCKS = 8
LRU_BLOCK = 128
LRU_CONV_W = 4
LRU_C = 8.0
D_FF = 3072
FFN_CONV_W = 3
RMS_EPS = 1e-6
NEG_INF = -1e30
PAGE_SIZE = 128

N_F32_BLOCKS = 5
N_IN_BLOCKS = 9
ZB_Q, ZB_K, ZB_V, ZB_CQ = 0, 1, 2, 3

LANE = 128
SUBLANE = 8
VMEM_LIMIT = 56 * 1024 * 1024


def _params(sem, vmem=VMEM_LIMIT):
    return pltpu.CompilerParams(dimension_semantics=sem, vmem_limit_bytes=vmem)


def _rms(x):
    return x * lax.rsqrt(jnp.mean(x * x, axis=-1, keepdims=True) + RMS_EPS)


def _dot(a, b):
    return jnp.dot(a, b, preferred_element_type=F32)


def _dot_nt(a, b):
    return lax.dot_general(a, b, (((1,), (1,)), ((), ())), preferred_element_type=F32)


def _dot_tn(a, b):
    return lax.dot_general(a, b, (((0,), (0,)), ((), ())), preferred_element_type=F32)


def _lam(q1, k1, q2, k2, lam_init):
    s1 = jnp.sum(q1 * k1, axis=-1, keepdims=True)
    s2 = jnp.sum(q2 * k2, axis=-1, keepdims=True)
    return jnp.exp(s1) - jnp.exp(s2) + lam_init


def _in_proj_kernel(x_ref, g_ref, w_ref, zf_ref, zb_ref, k_ref, v_ref, h_ref):
    n = pl.program_id(1)

    @pl.when(n == 0)
    def _():
        h_ref[...] = (_rms(x_ref[...]) * g_ref[...]).astype(BF16)

    z = _dot(h_ref[...], w_ref[...])

    @pl.when(n < N_F32_BLOCKS)
    def _():
        zf_ref[...] = z

    @pl.when(n >= N_F32_BLOCKS)
    def _():
        zb_ref[...] = z.astype(zb_ref.dtype)

    @pl.when(n == N_F32_BLOCKS + ZB_K)
    def _():
        k_ref[...] = z

    @pl.when(n == N_F32_BLOCKS + ZB_V)
    def _():
        v_ref[...] = z


def _in_proj(x, g, w_in, zb_dtype, tm):
    m = x.shape[0]
    nb = N_F32_BLOCKS
    return pl.pallas_call(
        _in_proj_kernel,
        grid=(m // tm, N_IN_BLOCKS),
        in_specs=[
            pl.BlockSpec((tm, D_MODEL), lambda i, n: (i, 0)),
            pl.BlockSpec((1, D_MODEL), lambda i, n: (0, 0)),
            pl.BlockSpec((D_MODEL, D_MODEL), lambda i, n: (0, n)),
        ],
        out_specs=[
            pl.BlockSpec((tm, D_MODEL), lambda i, n: (i, jnp.minimum(n, nb - 1))),
            pl.BlockSpec((tm, D_MODEL), lambda i, n: (i, jnp.maximum(n - nb, 0))),
            pl.BlockSpec((tm, D_MODEL), lambda i, n: (i, 0)),
            pl.BlockSpec((tm, D_MODEL), lambda i, n: (i, 0)),
        ],
        out_shape=[
            jax.ShapeDtypeStruct((m, nb * D_MODEL), F32),
            jax.ShapeDtypeStruct((m, (N_IN_BLOCKS - nb) * D_MODEL), zb_dtype),
            jax.ShapeDtypeStruct((m, D_MODEL), F32),
            jax.ShapeDtypeStruct((m, D_MODEL), F32),
        ],
        scratch_shapes=[pltpu.VMEM((tm, D_MODEL), BF16)],
        compiler_params=_params(("parallel", "arbitrary")),
        name="in_proj",
    )(x, g, w_in)


def _lru_kernel(lx_ref, ly_ref, s0_ref, h0_ref, wc_ref, bc_ref, wrg_ref, ba_ref, bi_ref,
                lam_ref, out_ref, hN_ref, cN_ref, xp_ref, a_ref, u_ref, hc_ref, *, bb, tt):
    t = pl.program_id(1)
    W = LRU_CONV_W
    P = SUBLANE

    @pl.when(t == 0)
    def _():
        xp_ref[:, P - (W - 1):P, :] = s0_ref[...]
        hc_ref[...] = h0_ref[...]

    xp_ref[:, P:, :] = lx_ref[...]
    xc = xp_ref[:, P - 3:P - 3 + tt, :] * wc_ref[0:1, :]
    for j in range(1, W):
        xc = xc + xp_ref[:, P - 3 + j:P - 3 + j + tt, :] * wc_ref[j:j + 1, :]
    xc = xc + bc_ref[...]
    tail = xp_ref[:, tt + P - (W - 1):tt + P, :]
    xp_ref[:, P - (W - 1):P, :] = tail
    cN_ref[...] = tail

    x2 = xc.reshape(bb * tt, D_MODEL)
    xb = x2.astype(BF16)
    lin = [_dot(xb[:, n * LRU_BLOCK:(n + 1) * LRU_BLOCK], wrg_ref[n])
           for n in range(N_LRU_BLOCKS)]
    r_lin = jnp.concatenate([z[:, :LRU_BLOCK] for z in lin], axis=-1)
    i_lin = jnp.concatenate([z[:, LRU_BLOCK:] for z in lin], axis=-1)
    r = jax.nn.sigmoid(r_lin + ba_ref[...])
    gi = jax.nn.sigmoid(i_lin + bi_ref[...])
    neg_lam = -lam_ref[...]
    softplus = jnp.maximum(neg_lam, 0.0) + jnp.log1p(jnp.exp(-jnp.abs(neg_lam)))
    log_a = (-LRU_C * softplus) * r
    a = jnp.exp(log_a)
    u = jnp.sqrt(1.0 - a * a) * (gi * x2)

    g8 = (bb * tt) // SUBLANE
    a3 = a.reshape(g8, SUBLANE, D_MODEL)
    u3 = u.reshape(g8, SUBLANE, D_MODEL)
    row = lax.broadcasted_iota(jnp.int32, (g8, SUBLANE, D_MODEL), 1)
    for s in (1, 2, 4):
        a_sh = pltpu.roll(a3, s, 1)
        u_sh = pltpu.roll(u3, s, 1)
        ok = row >= s
        u3 = jnp.where(ok, a3 * u_sh + u3, u3)
        a3 = jnp.where(ok, a3 * a_sh, a3)
    a_ref[...] = a3.reshape(bb, tt, D_MODEL)
    u_ref[...] = u3.reshape(bb, tt, D_MODEL)

    def body(g, h):
        off = pl.multiple_of(g * SUBLANE, SUBLANE)
        hg = a_ref[:, pl.ds(off, SUBLANE), :] * h + u_ref[:, pl.ds(off, SUBLANE), :]
        u_ref[:, pl.ds(off, SUBLANE), :] = hg
        return hg[:, SUBLANE - 1:SUBLANE, :]

    h_last = lax.fori_loop(0, tt // SUBLANE, body, hc_ref[...])
    hc_ref[...] = h_last
    hN_ref[...] = h_last
    out_ref[...] = (jax.nn.gelu(ly_ref[...]) * u_ref[...]).astype(out_ref.dtype)


def _lru(zf3, s0, h0, wc, bc, wrg, ba, bi, lam, out_dtype, bb, tt):
    b, t, _ = zf3.shape
    kern = functools.partial(_lru_kernel, bb=bb, tt=tt)
    const2 = lambda i, j: (0, 0)
    return pl.pallas_call(
        kern,
        grid=(b // bb, t // tt),
        in_specs=[
            pl.BlockSpec((bb, tt, D_MODEL), lambda i, j: (i, j, 0)),
            pl.BlockSpec((bb, tt, D_MODEL), lambda i, j: (i, j, 1)),
            pl.BlockSpec((bb, LRU_CONV_W - 1, D_MODEL), lambda i, j: (i, 0, 0)),
            pl.BlockSpec((bb, 1, D_MODEL), lambda i, j: (i, 0, 0)),
            pl.BlockSpec((LRU_CONV_W, D_MODEL), const2),
            pl.BlockSpec((1, D_MODEL), const2),
            pl.BlockSpec((N_LRU_BLOCKS, LRU_BLOCK, 2 * LRU_BLOCK), lambda i, j: (0, 0, 0)),
            pl.BlockSpec((1, D_MODEL), const2),
            pl.BlockSpec((1, D_MODEL), const2),
            pl.BlockSpec((1, D_MODEL), const2),
        ],
        out_specs=[
            pl.BlockSpec((bb, tt, D_MODEL), lambda i, j: (i, j, 0)),
            pl.BlockSpec((bb, 1, D_MODEL), lambda i, j: (i, 0, 0)),
            pl.BlockSpec((bb, LRU_CONV_W - 1, D_MODEL), lambda i, j: (i, 0, 0)),
        ],
        out_shape=[
            jax.ShapeDtypeStruct((b, t, D_MODEL), out_dtype),
            jax.ShapeDtypeStruct((b, 1, D_MODEL), F32),
            jax.ShapeDtypeStruct((b, LRU_CONV_W - 1, D_MODEL), F32),
        ],
        scratch_shapes=[
            pltpu.VMEM((bb, tt + SUBLANE, D_MODEL), F32),
            pltpu.VMEM((bb, tt, D_MODEL), F32),
            pltpu.VMEM((bb, tt, D_MODEL), F32),
            pltpu.VMEM((bb, 1, D_MODEL), F32),
        ],
        compiler_params=_params(("parallel", "arbitrary")),
        name="rg_lru",
    )(zf3, zf3, s0, h0, wc, bc, wrg, ba, bi, lam)


def _attn_prompt_kernel(q_ref, k_ref, v_ref, lq1_ref, lk1_ref, lq2_ref, lk2_ref, gs_ref,
                        o_ref, m_ref, l_ref, acc_ref, *, tq, lam_init):
    qi = pl.program_id(2)
    q = q_ref[...]
    lane = lax.broadcasted_iota(jnp.int32, q.shape, 1)
    zero = jnp.zeros_like(q)
    qq = jnp.concatenate([jnp.where(lane < HEAD_DIM, q, zero),
                          jnp.where(lane >= HEAD_DIM, q, zero)], axis=0)

    m_ref[...] = jnp.full(m_ref.shape, NEG_INF, F32)
    l_ref[...] = jnp.zeros(l_ref.shape, F32)
    acc_ref[...] = jnp.zeros(acc_ref.shape, F32)

    def step(ki, diagonal):
        off = pl.multiple_of(ki * tq, tq)
        kt = k_ref[pl.ds(off, tq), :]
        vt = v_ref[pl.ds(off, tq), :]
        s = _dot_nt(kt, qq)
        if diagonal:
            key = lax.broadcasted_iota(jnp.int32, s.shape, 0)
            qry = lax.broadcasted_iota(jnp.int32, s.shape, 1)
            qry = jnp.where(qry >= tq, qry - tq, qry)
            s = jnp.where(key <= qry, s, NEG_INF)
        m_old = m_ref[...]
        m_new = jnp.maximum(m_old, jnp.max(s, axis=0, keepdims=True))
        alpha = jnp.exp(m_old - m_new)
        p = jnp.exp(s - m_new)
        l_ref[...] = alpha * l_ref[...] + jnp.sum(p, axis=0, keepdims=True)
        acc_ref[...] = alpha * acc_ref[...] + _dot_tn(vt, p.astype(BF16))
        m_ref[...] = m_new

    def body(ki, c):
        step(ki, False)
        return c

    lax.fori_loop(0, qi, body, 0)
    step(qi, True)

    lam = _lam(lq1_ref[...], lk1_ref[...], lq2_ref[...], lk2_ref[...], lam_init)
    inv_l = 1.0 / l_ref[...]
    o_t = acc_ref[...] * inv_l
    o_t = o_t[:, :tq] - lam * o_t[:, tq:]
    o = o_t.T
    o_ref[...] = ((_rms(o) * gs_ref[...]) * (1.0 - lam_init)).astype(o_ref.dtype)


def _attn_prompt(zb, lq1, lk1, lq2, lk2, gs, b, t, tq, lam_init):
    m = zb.shape[0]
    nq = t // tq
    kern = functools.partial(_attn_prompt_kernel, tq=tq, lam_init=lam_init)
    vec = lambda i, h, j: (0, 0)
    return pl.pallas_call(
        kern,
        grid=(b, N_HEADS, nq),
        in_specs=[
            pl.BlockSpec((tq, V_DIM), lambda i, h, j: (i * nq + j, ZB_Q * N_HEADS + h)),
            pl.BlockSpec((t, V_DIM), lambda i, h, j: (i, ZB_K * N_HEADS + h)),
            pl.BlockSpec((t, V_DIM), lambda i, h, j: (i, ZB_V * N_HEADS + h)),
            pl.BlockSpec((1, HEAD_DIM), vec),
            pl.BlockSpec((1, HEAD_DIM), vec),
            pl.BlockSpec((1, HEAD_DIM), vec),
            pl.BlockSpec((1, HEAD_DIM), vec),
            pl.BlockSpec((1, V_DIM), vec),
        ],
        out_specs=pl.BlockSpec((tq, V_DIM), lambda i, h, j: (i * nq + j, h)),
        out_shape=jax.ShapeDtypeStruct((m, N_HEADS * V_DIM), BF16),
        scratch_shapes=[
            pltpu.VMEM((1, 2 * tq), F32),
            pltpu.VMEM((1, 2 * tq), F32),
            pltpu.VMEM((V_DIM, 2 * tq), F32),
        ],
        compiler_params=_params(("parallel", "parallel", "arbitrary")),
        name="diff_attn_prompt",
    )(zb, zb, zb, lq1, lk1, lq2, lk2, gs)


def _attn_sample_kernel(pt_ref, q_ref, kn_ref, vn_ref, *rest, n_pages, tn, lam_init):
    del pt_ref
    kp_refs = rest[:n_pages]
    vp_refs = rest[n_pages:2 * n_pages]
    lq1_ref, lk1_ref, lq2_ref, lk2_ref, gs_ref, o_ref, s_ref = rest[2 * n_pages:]
    ncol = 2 * N_HEADS * tn
    n_past = n_pages * PAGE_SIZE

    q = q_ref[0]
    qt = jnp.concatenate([q] * (ncol // tn), axis=0)
    rj = lax.broadcasted_iota(jnp.int32, qt.shape, 0)
    cf = lax.broadcasted_iota(jnp.int32, qt.shape, 1)
    lt = tn.bit_length() - 1
    lh = N_HEADS.bit_length() - 1
    comp_j = rj >> (lt + lh)
    head_j = (rj >> lt) & (N_HEADS - 1)
    feat_blk = cf >> (HEAD_DIM.bit_length() - 1)
    qbd = jnp.where(feat_blk == 2 * head_j + comp_j, qt, 0.0)

    for p in range(n_pages):
        s_ref[p * PAGE_SIZE:(p + 1) * PAGE_SIZE, :] = _dot_nt(kp_refs[p][0], qbd)
    s_new = _dot_nt(kn_ref[0], qbd)
    key = lax.broadcasted_iota(jnp.int32, s_new.shape, 0)
    qry = lax.broadcasted_iota(jnp.int32, s_new.shape, 1) & (tn - 1)
    s_ref[n_past:n_past + tn, :] = jnp.where(key <= qry, s_new, NEG_INF)

    s = s_ref[...]
    mx = jnp.max(s, axis=0, keepdims=True)
    e = jnp.exp(s - mx)
    p_all = e / jnp.sum(e, axis=0, keepdims=True)
    s_ref[...] = p_all

    acc = _dot_tn(s_ref[n_past:n_past + tn, :], vn_ref[0])
    for p in range(n_pages):
        pp = s_ref[p * PAGE_SIZE:(p + 1) * PAGE_SIZE, :]
        acc = acc + _dot_tn(pp, vp_refs[p][0])

    lam = _lam(lq1_ref[...], lk1_ref[...], lq2_ref[...], lk2_ref[...], lam_init)
    half = N_HEADS * tn
    comb = acc[:half, :] - lam * acc[half:, :]
    outs = []
    for h in range(N_HEADS):
        o = comb[h * tn:(h + 1) * tn, h * V_DIM:(h + 1) * V_DIM]
        outs.append((_rms(o) * gs_ref[...]) * (1.0 - lam_init))
    o_ref[0] = jnp.concatenate(outs, axis=-1).astype(o_ref.dtype)


def _attn_sample(page_table, zb3, k3, v3, cache_k, cache_v, lq1, lk1, lq2, lk2, gs, lam_init):
    bd, tn, _ = k3.shape
    n_pages = page_table.shape[1]
    kern = functools.partial(_attn_sample_kernel, n_pages=n_pages, tn=tn, lam_init=lam_init)
    vec = lambda i, pt: (0, 0)

    def page_spec(p):
        return pl.BlockSpec((1, PAGE_SIZE, D_MODEL), lambda i, pt: (pt[i, p], 0, 0))

    grid_spec = pltpu.PrefetchScalarGridSpec(
        num_scalar_prefetch=1,
        grid=(bd,),
        in_specs=[
            pl.BlockSpec((1, tn, D_MODEL), lambda i, pt: (i, 0, ZB_Q)),
            pl.BlockSpec((1, tn, D_MODEL), lambda i, pt: (i, 0, 0)),
            pl.BlockSpec((1, tn, D_MODEL), lambda i, pt: (i, 0, 0)),
            *[page_spec(p) for p in range(n_pages)],
            *[page_spec(p) for p in range(n_pages)],
            pl.BlockSpec((1, HEAD_DIM), vec),
            pl.BlockSpec((1, HEAD_DIM), vec),
            pl.BlockSpec((1, HEAD_DIM), vec),
            pl.BlockSpec((1, HEAD_DIM), vec),
            pl.BlockSpec((1, V_DIM), vec),
        ],
        out_specs=pl.BlockSpec((1, tn, D_MODEL), lambda i, pt: (i, 0, 0)),
        scratch_shapes=[pltpu.VMEM((n_pages * PAGE_SIZE + tn, 2 * N_HEADS * tn), F32)],
    )
    return pl.pallas_call(
        kern,
        grid_spec=grid_spec,
        out_shape=jax.ShapeDtypeStruct((bd, tn, D_MODEL), F32),
        compiler_params=_params(("arbitrary",)),
        name="diff_attn_sample",
    )(page_table, zb3, k3, v3, *([cache_k] * n_pages), *([cache_v] * n_pages),
      lq1, lk1, lq2, lk2, gs)


def _mem_kv_kernel(x_ref, g_ref, w_ref, o_ref):
    h = (_rms(x_ref[...]) * g_ref[...]).astype(BF16)
    o_ref[0] = _dot(h, w_ref[...])


def _mem_kv(mem2, g, w):
    m = mem2.shape[0]
    n = w.shape[1]
    return pl.pallas_call(
        _mem_kv_kernel,
        grid=(n // D_MODEL,),
        in_specs=[
            pl.BlockSpec((m, D_MODEL), lambda j: (0, 0)),
            pl.BlockSpec((1, D_MODEL), lambda j: (0, 0)),
            pl.BlockSpec((D_MODEL, D_MODEL), lambda j: (0, j)),
        ],
        out_specs=pl.BlockSpec((1, m, D_MODEL), lambda j: (j, 0, 0)),
        out_shape=jax.ShapeDtypeStruct((n // D_MODEL, m, D_MODEL), F32),
        compiler_params=_params(("parallel",)),
        name="mem_kv",
    )(mem2, g, w)


def _mem_attn_kernel(q_ref, mk_ref, mv_ref, o_ref):
    mxu_dtype = q_ref.dtype
    q = q_ref[0]
    mk = mk_ref[0].astype(mxu_dtype)
    mv = mv_ref[0].astype(mxu_dtype)
    outs = []
    for h in range(MEM_HEADS):
        sl = slice(h * MEM_HEAD_DIM, (h + 1) * MEM_HEAD_DIM)
        s = _dot_nt(q[:, sl], mk[:, sl])
        e = jnp.exp(s - jnp.max(s, axis=-1, keepdims=True))
        p = e / jnp.sum(e, axis=-1, keepdims=True)
        outs.append(_dot(p.astype(mxu_dtype), mv[:, sl]))
    o_ref[0] = jnp.concatenate(outs, axis=-1).astype(o_ref.dtype)


def _mem_attn(zb3, mk3, mv3, out_dtype, tq):
    b, t, _ = zb3.shape
    return pl.pallas_call(
        _mem_attn_kernel,
        grid=(b, t // tq),
        in_specs=[
            pl.BlockSpec((1, tq, D_MODEL), lambda i, j: (i, j, ZB_CQ)),
            pl.BlockSpec((1, N_MEM, D_MODEL), lambda i, j: (i, 0, 0)),
            pl.BlockSpec((1, N_MEM, D_MODEL), lambda i, j: (i, 0, 0)),
        ],
        out_specs=pl.BlockSpec((1, tq, D_MODEL), lambda i, j: (i, j, 0)),
        out_shape=jax.ShapeDtypeStruct((b, t, D_MODEL), out_dtype),
        compiler_params=_params(("parallel", "arbitrary")),
        name="mem_attn",
    )(zb3, mk3, mv3)


def _merge_kernel(bl_ref, ba_ref, bm_ref, g0_ref, g1_ref, g2_ref, x_ref,
                  w0_ref, w1_ref, w2_ref, wo_ref, gpost_ref, gpre_ref, x1_ref, h2_ref):
    m = jax.nn.sigmoid(g0_ref[...]) * _dot(bl_ref[...].astype(BF16), w0_ref[...])
    m = m + jax.nn.sigmoid(g1_ref[...]) * _dot(ba_ref[...].astype(BF16), w1_ref[...])
    m = m + jax.nn.sigmoid(g2_ref[...]) * _dot(bm_ref[...].astype(BF16), w2_ref[...])
    y = _dot(m.astype(BF16), wo_ref[...])
    x1 = x_ref[...] + _rms(y) * gpost_ref[...]
    x1_ref[...] = x1
    h2_ref[...] = (_rms(x1) * gpre_ref[...]).astype(h2_ref.dtype)


def _merge(br_lru, br_attn, br_mem, zf, x, w0, w1, w2, wo, gpost, gpre, h2_dtype, tm):
    m = x.shape[0]
    tok = lambda i: (i, 0)
    cst = lambda i: (0, 0)
    wspec = pl.BlockSpec((D_MODEL, D_MODEL), cst)
    return pl.pallas_call(
        _merge_kernel,
        grid=(m // tm,),
        in_specs=[
            pl.BlockSpec((tm, D_MODEL), tok),
            pl.BlockSpec((tm, D_MODEL), tok),
            pl.BlockSpec((tm, D_MODEL), tok),
            pl.BlockSpec((tm, D_MODEL), lambda i: (i, 2)),
            pl.BlockSpec((tm, D_MODEL), lambda i: (i, 3)),
            pl.BlockSpec((tm, D_MODEL), lambda i: (i, 4)),
            pl.BlockSpec((tm, D_MODEL), tok),
            wspec, wspec, wspec, wspec,
            pl.BlockSpec((1, D_MODEL), cst),
            pl.BlockSpec((1, D_MODEL), cst),
        ],
        out_specs=[pl.BlockSpec((tm, D_MODEL), tok), pl.BlockSpec((tm, D_MODEL), tok)],
        out_shape=[jax.ShapeDtypeStruct((m, D_MODEL), F32),
                   jax.ShapeDtypeStruct((m, D_MODEL), h2_dtype)],
        compiler_params=_params(("parallel",)),
        name="merge_out_proj",
    )(br_lru, br_attn, br_mem, zf, zf, zf, x, w0, w1, w2, wo, gpost, gpre)


FFN_CHUNK = 512


def _ffn_kernel(h_ref, x_ref, s0_ref, wup_ref, wc_ref, bc_ref, wdn_ref, g_ref,
                y_ref, cN_ref, up_ref, *, bb, tt):
    t = pl.program_id(1)
    W = FFN_CONV_W
    P = SUBLANE
    rows = bb * tt

    @pl.when(t == 0)
    def _():
        up_ref[:, P - (W - 1):P, :] = s0_ref[...]

    h = h_ref[...].reshape(rows, D_MODEL).astype(BF16)
    acc = jnp.zeros((rows, D_MODEL), F32)
    for c in range(D_FF // FFN_CHUNK):
        halves = []
        for base in (0, D_FF):
            cs = slice(base + c * FFN_CHUNK, base + (c + 1) * FFN_CHUNK)
            up_ref[:, P:, cs] = _dot(h, wup_ref[:, cs]).reshape(bb, tt, FFN_CHUNK)
            uc = up_ref[:, P - 2:P - 2 + tt, cs] * wc_ref[0:1, cs]
            for j in range(1, W):
                uc = uc + up_ref[:, P - 2 + j:P - 2 + j + tt, cs] * wc_ref[j:j + 1, cs]
            halves.append((uc + bc_ref[:, cs]).reshape(rows, FFN_CHUNK))
        act = (jax.nn.gelu(halves[0]) * halves[1]).astype(BF16)
        acc = acc + _dot(act, wdn_ref[c * FFN_CHUNK:(c + 1) * FFN_CHUNK, :])

    tail = up_ref[:, tt + P - (W - 1):tt + P, :]
    up_ref[:, P - (W - 1):P, :] = tail
    cN_ref[...] = tail
    y = x_ref[...].reshape(rows, D_MODEL) + _rms(acc) * g_ref[...]
    y_ref[...] = y.reshape(bb, tt, D_MODEL)


def _ffn(h3, x3, s0, wup, wc, bc, wdn, g, bb, tt):
    b, t, _ = x3.shape
    kern = functools.partial(_ffn_kernel, bb=bb, tt=tt)
    cst = lambda i, j: (0, 0)
    once = pl.Buffered(1)
    return pl.pallas_call(
        kern,
        grid=(b // bb, t // tt),
        in_specs=[
            pl.BlockSpec((bb, tt, D_MODEL), lambda i, j: (i, j, 0)),
            pl.BlockSpec((bb, tt, D_MODEL), lambda i, j: (i, j, 0)),
            pl.BlockSpec((bb, FFN_CONV_W - 1, 2 * D_FF), lambda i, j: (i, 0, 0)),
            pl.BlockSpec((D_MODEL, 2 * D_FF), cst, pipeline_mode=once),
            pl.BlockSpec((FFN_CONV_W, 2 * D_FF), cst),
            pl.BlockSpec((1, 2 * D_FF), cst),
            pl.BlockSpec((D_FF, D_MODEL), cst, pipeline_mode=once),
            pl.BlockSpec((1, D_MODEL), cst),
        ],
        out_specs=[
            pl.BlockSpec((bb, tt, D_MODEL), lambda i, j: (i, j, 0)),
            pl.BlockSpec((bb, FFN_CONV_W - 1, 2 * D_FF), lambda i, j: (i, 0, 0)),
        ],
        out_shape=[
            jax.ShapeDtypeStruct((b, t, D_MODEL), F32),
            jax.ShapeDtypeStruct((b, FFN_CONV_W - 1, 2 * D_FF), F32),
        ],
        scratch_shapes=[pltpu.VMEM((bb, tt + SUBLANE, 2 * D_FF), F32)],
        compiler_params=_params(("parallel", "arbitrary")),
        name="conv_ffn",
    )(h3, x3, s0, wup, wc, bc, wdn, g)


def _layer(x3, attn_fn, mk3, mv3, lru_h0, lru_conv0, ffn_conv0, wts, *, prompt):
    b, t, _ = x3.shape
    m = b * t
    x2 = x3.reshape(m, D_MODEL)
    act_dtype = BF16 if prompt else F32
    tm = 512
    zf, zb, k, v = _in_proj(x2, wts["g_pre_mix"], wts["w_in"], act_dtype, tm)

    bb, tt = (1, 512) if prompt else (32, t)
    br_lru, lru_h, lru_conv = _lru(
        zf.reshape(b, t, -1), lru_conv0, lru_h0.reshape(b, 1, D_MODEL),
        wts["w_lru_conv"], wts["b_lru_conv"], wts["w_rg"], wts["b_rg_a"], wts["b_rg_i"],
        wts["lru_lambda"], act_dtype, bb, tt)

    br_attn = attn_fn(zb, k, v)
    br_mem = _mem_attn(zb.reshape(b, t, -1), mk3, mv3, act_dtype, 512 if prompt else t)

    x1, h2 = _merge(br_lru.reshape(m, D_MODEL), br_attn.reshape(m, D_MODEL),
                    br_mem.reshape(m, D_MODEL), zf, x2,
                    wts["w_br_lru"], wts["w_br_attn"], wts["w_br_mem"], wts["w_out"],
                    wts["g_post_mix"], wts["g_pre_ffn"], act_dtype, tm)

    bb, tt = (1, 256) if prompt else (16, t)
    y, ffn_conv = _ffn(h2.reshape(b, t, D_MODEL), x1.reshape(b, t, D_MODEL), ffn_conv0,
                       wts["w_up"], wts["w_ffn_conv"], wts["b_ffn_conv"], wts["w_down"],
                       wts["g_post_ffn"], bb, tt)
    return y, k, v, lru_h.reshape(b, D_MODEL), lru_conv, ffn_conv


def _lambda_init(layer_idx):
    return 0.8 - 0.6 * math.exp(-0.3 * layer_idx)


def kernel(x_prompt, x_sample, mem_prompt, cache_k, cache_v, page_table, cache_mem_k, cache_mem_v, state_lru_h, state_lru_conv, state_ffn_conv, g_pre_mix, w_in, w_lru_conv, b_lru_conv, w_rg_a, b_rg_a, w_rg_i, b_rg_i, lru_lambda, lambda_q1, lambda_k1, lambda_q2, lambda_k2, g_subln, g_mem, w_mem_kv, w_br_lru, w_br_attn, w_br_mem, w_out, g_post_mix, g_pre_ffn, w_up, w_ffn_conv, b_ffn_conv, w_down, g_post_ffn):
    depth = w_in.shape[0]
    bp, tp, _ = x_prompt.shape
    bd, td, _ = x_sample.shape
    xp, xs = x_prompt, x_sample
    outs = [[] for _ in range(12)]
    row = lambda a: a.reshape(1, -1)

    for l in range(depth):
        lam_init = _lambda_init(l)
        wi = w_in[l]
        blk = lambda i: wi[:, i * D_MODEL:(i + 1) * D_MODEL]
        w_in_p = jnp.concatenate(
            [blk(0), blk(1), blk(6), blk(7), blk(8),
             blk(2) * (1.0 / math.sqrt(HEAD_DIM)), blk(3), blk(4),
             blk(5) * (1.0 / math.sqrt(MEM_HEAD_DIM))], axis=1).astype(BF16)
        wts = dict(
            g_pre_mix=row(g_pre_mix[l]), w_in=w_in_p,
            w_lru_conv=w_lru_conv[l], b_lru_conv=row(b_lru_conv[l]),
            w_rg=jnp.concatenate([w_rg_a[l], w_rg_i[l]], axis=-1).astype(BF16),
            b_rg_a=row(b_rg_a[l]), b_rg_i=row(b_rg_i[l]), lru_lambda=row(lru_lambda[l]),
            w_br_lru=w_br_lru[l].astype(BF16), w_br_attn=w_br_attn[l].astype(BF16),
            w_br_mem=w_br_mem[l].astype(BF16), w_out=w_out[l].astype(BF16),
            g_post_mix=row(g_post_mix[l]), g_pre_ffn=row(g_pre_ffn[l]),
            w_up=w_up[l].astype(BF16), w_ffn_conv=w_ffn_conv[l], b_ffn_conv=row(b_ffn_conv[l]),
            w_down=w_down[l].astype(BF16), g_post_ffn=row(g_post_ffn[l]),
        )
        lams = (row(lambda_q1[l]), row(lambda_k1[l]), row(lambda_q2[l]), row(lambda_k2[l]))
        gs = row(g_subln[l])

        mkv = _mem_kv(mem_prompt.reshape(bp * N_MEM, D_MODEL), row(g_mem[l]),
                      w_mem_kv[l].astype(BF16))
        mk_p = mkv[0].reshape(bp, N_MEM, D_MODEL)
        mv_p = mkv[1].reshape(bp, N_MEM, D_MODEL)
        attn_p = lambda zb, k, v: _attn_prompt(zb, *lams, gs, bp, tp, 256, lam_init)
        xp, k_p, v_p, h_p, c_p, f_p = _layer(
            xp, attn_p, mk_p, mv_p,
            jnp.zeros((bp, D_MODEL), F32),
            jnp.zeros((bp, LRU_CONV_W - 1, D_MODEL), F32),
            jnp.zeros((bp, FFN_CONV_W - 1, 2 * D_FF), F32),
            wts, prompt=True)

        ck = cache_k[l].reshape(-1, PAGE_SIZE, D_MODEL)
        cv = cache_v[l].reshape(-1, PAGE_SIZE, D_MODEL)
        attn_s = lambda zb, k, v: _attn_sample(
            page_table, zb.reshape(bd, td, -1), k.reshape(bd, td, D_MODEL),
            v.reshape(bd, td, D_MODEL), ck, cv, *lams, gs, lam_init)
        xs, k_s, v_s, h_s, c_s, f_s = _layer(
            xs, attn_s, cache_mem_k[l].reshape(bd, N_MEM, D_MODEL),
            cache_mem_v[l].reshape(bd, N_MEM, D_MODEL),
            state_lru_h[l], state_lru_conv[l], state_ffn_conv[l], wts, prompt=False)

        vals = (k_p.reshape(bp, tp, N_HEADS, 2, HEAD_DIM), v_p.reshape(bp, tp, N_HEADS, V_DIM),
                mk_p.reshape(bp, N_MEM, MEM_HEADS, MEM_HEAD_DIM),
                mv_p.reshape(bp, N_MEM, MEM_HEADS, MEM_HEAD_DIM), h_p, c_p, f_p,
                k_s.reshape(bd, td, N_HEADS, 2, HEAD_DIM), v_s.reshape(bd, td, N_HEADS, V_DIM),
                h_s, c_s, f_s)
        for o, val in zip(outs, vals):
            o.append(val)

    return (xp, xs, *[jnp.stack(o) for o in outs])
```

```python
import functools
import math

import jax
import jax.numpy as jnp
from jax import lax
from jax.experimental import pallas as pl
from jax.experimental.pallas import tpu as pltpu

F32 = jnp.float32
BF16 = jnp.bfloat16

D_MODEL = 1024
N_HEADS = 8
HEAD_DIM = 64
V_DIM = 128
N_MEM = 256
MEM_HEADS = 4
MEM_HEAD_DIM = 256
N_LRU_BLOCKS = 8
LRU_BLOCK = 128
LRU_CONV_W = 4
LRU_C = 8.0
D_FF = 3072
FFN_CONV_W = 3
RMS_EPS = 1e-6
NEG_INF = -1e30
LOG2_E = math.log2(math.e)
PAGE_SIZE = 128

N_F32_BLOCKS = 5
N_IN_BLOCKS = 9
ZB_Q, ZB_K, ZB_V, ZB_CQ = 0, 1, 2, 3

LANE = 128
SUBLANE = 8
VMEM_LIMIT = 56 * 1024 * 1024
ATTN_TQ = 512
ATTN_HP = 4
ATTN_CW = 256


def _params(sem, vmem=VMEM_LIMIT):
    return pltpu.CompilerParams(dimension_semantics=sem, vmem_limit_bytes=vmem)


def _rms(x):
    return x * lax.rsqrt(jnp.mean(x * x, axis=-1, keepdims=True) + RMS_EPS)


def _dot(a, b):
    return jnp.dot(a, b, preferred_element_type=F32)


def _dot_nt(a, b):
    return lax.dot_general(a, b, (((1,), (1,)), ((), ())), preferred_element_type=F32)


def _dot_tn(a, b):
    return lax.dot_general(a, b, (((0,), (0,)), ((), ())), preferred_element_type=F32)


def _lam(q1, k1, q2, k2, lam_init):
    s1 = jnp.sum(q1 * k1, axis=-1, keepdims=True)
    s2 = jnp.sum(q2 * k2, axis=-1, keepdims=True)
    return jnp.exp(s1) - jnp.exp(s2) + lam_init


def _in_proj_kernel(x_ref, g_ref, w_ref, zf_ref, zb_ref, k_ref, v_ref, h_ref):
    n = pl.program_id(1)

    @pl.when(n == 0)
    def _():
        h_ref[...] = (_rms(x_ref[...]) * g_ref[...]).astype(BF16)

    z = _dot(h_ref[...], w_ref[...])

    @pl.when(n < N_F32_BLOCKS)
    def _():
        zf_ref[...] = z

    @pl.when(n >= N_F32_BLOCKS)
    def _():
        zb_ref[...] = z.astype(zb_ref.dtype)

    @pl.when(n == N_F32_BLOCKS + ZB_K)
    def _():
        k_ref[...] = z

    @pl.when(n == N_F32_BLOCKS + ZB_V)
    def _():
        v_ref[...] = z


def _in_proj(x, g, w_in, zb_dtype, tm):
    m = x.shape[0]
    nb = N_F32_BLOCKS
    return pl.pallas_call(
        _in_proj_kernel,
        grid=(m // tm, N_IN_BLOCKS),
        in_specs=[
            pl.BlockSpec((tm, D_MODEL), lambda i, n: (i, 0)),
            pl.BlockSpec((1, D_MODEL), lambda i, n: (0, 0)),
            pl.BlockSpec((D_MODEL, D_MODEL), lambda i, n: (0, n)),
        ],
        out_specs=[
            pl.BlockSpec((tm, D_MODEL), lambda i, n: (i, jnp.minimum(n, nb - 1))),
            pl.BlockSpec((tm, D_MODEL), lambda i, n: (i, jnp.maximum(n - nb, 0))),
            pl.BlockSpec((tm, D_MODEL), lambda i, n: (i, 0)),
            pl.BlockSpec((tm, D_MODEL), lambda i, n: (i, 0)),
        ],
        out_shape=[
            jax.ShapeDtypeStruct((m, nb * D_MODEL), F32),
            jax.ShapeDtypeStruct((m, (N_IN_BLOCKS - nb) * D_MODEL), zb_dtype),
            jax.ShapeDtypeStruct((m, D_MODEL), F32),
            jax.ShapeDtypeStruct((m, D_MODEL), F32),
        ],
        scratch_shapes=[pltpu.VMEM((tm, D_MODEL), BF16)],
        compiler_params=_params(("parallel", "arbitrary")),
        name="in_proj",
    )(x, g, w_in)


def _lru_kernel(lx_ref, ly_ref, s0_ref, h0_ref, wc_ref, bc_ref, wrg_ref, ba_ref, bi_ref,
                lam_ref, out_ref, hN_ref, cN_ref, xp_ref, a_ref, u_ref, hc_ref, *, bb, tt):
    t = pl.program_id(1)
    W = LRU_CONV_W
    P = SUBLANE

    @pl.when(t == 0)
    def _():
        xp_ref[:, P - (W - 1):P, :] = s0_ref[...]
        hc_ref[...] = h0_ref[...]

    xp_ref[:, P:, :] = lx_ref[...]
    xc = xp_ref[:, P - 3:P - 3 + tt, :] * wc_ref[0:1, :]
    for j in range(1, W):
        xc = xc + xp_ref[:, P - 3 + j:P - 3 + j + tt, :] * wc_ref[j:j + 1, :]
    xc = xc + bc_ref[...]
    tail = xp_ref[:, tt + P - (W - 1):tt + P, :]
    xp_ref[:, P - (W - 1):P, :] = tail
    cN_ref[...] = tail

    x2 = xc.reshape(bb * tt, D_MODEL)
    xb = x2.astype(BF16)
    lin = [_dot(xb[:, n * LRU_BLOCK:(n + 1) * LRU_BLOCK], wrg_ref[n])
           for n in range(N_LRU_BLOCKS)]
    r_lin = jnp.concatenate([z[:, :LRU_BLOCK] for z in lin], axis=-1)
    i_lin = jnp.concatenate([z[:, LRU_BLOCK:] for z in lin], axis=-1)
    r = jax.nn.sigmoid(r_lin + ba_ref[...])
    gi = jax.nn.sigmoid(i_lin + bi_ref[...])
    neg_lam = -lam_ref[...]
    softplus = jnp.maximum(neg_lam, 0.0) + jnp.log1p(jnp.exp(-jnp.abs(neg_lam)))
    log_a = (-LRU_C * softplus) * r
    a = jnp.exp(log_a)
    u = jnp.sqrt(1.0 - a * a) * (gi * x2)

    g8 = (bb * tt) // SUBLANE
    a3 = a.reshape(g8, SUBLANE, D_MODEL)
    u3 = u.reshape(g8, SUBLANE, D_MODEL)
    row = lax.broadcasted_iota(jnp.int32, (g8, SUBLANE, D_MODEL), 1)
    for s in (1, 2, 4):
        a_sh = pltpu.roll(a3, s, 1)
        u_sh = pltpu.roll(u3, s, 1)
        ok = row >= s
        u3 = jnp.where(ok, a3 * u_sh + u3, u3)
        a3 = jnp.where(ok, a3 * a_sh, a3)
    a_ref[...] = a3.reshape(bb, tt, D_MODEL)
    u_ref[...] = u3.reshape(bb, tt, D_MODEL)

    def body(g, h):
        off = pl.multiple_of(g * SUBLANE, SUBLANE)
        hg = a_ref[:, pl.ds(off, SUBLANE), :] * h + u_ref[:, pl.ds(off, SUBLANE), :]
        u_ref[:, pl.ds(off, SUBLANE), :] = hg
        return hg[:, SUBLANE - 1:SUBLANE, :]

    h_last = lax.fori_loop(0, tt // SUBLANE, body, hc_ref[...])
    hc_ref[...] = h_last
    hN_ref[...] = h_last
    out_ref[...] = (jax.nn.gelu(ly_ref[...]) * u_ref[...]).astype(out_ref.dtype)


def _lru(zf3, s0, h0, wc, bc, wrg, ba, bi, lam, out_dtype, bb, tt):
    b, t, _ = zf3.shape
    kern = functools.partial(_lru_kernel, bb=bb, tt=tt)
    const2 = lambda i, j: (0, 0)
    return pl.pallas_call(
        kern,
        grid=(b // bb, t // tt),
        in_specs=[
            pl.BlockSpec((bb, tt, D_MODEL), lambda i, j: (i, j, 0)),
            pl.BlockSpec((bb, tt, D_MODEL), lambda i, j: (i, j, 1)),
            pl.BlockSpec((bb, LRU_CONV_W - 1, D_MODEL), lambda i, j: (i, 0, 0)),
            pl.BlockSpec((bb, 1, D_MODEL), lambda i, j: (i, 0, 0)),
            pl.BlockSpec((LRU_CONV_W, D_MODEL), const2),
            pl.BlockSpec((1, D_MODEL), const2),
            pl.BlockSpec((N_LRU_BLOCKS, LRU_BLOCK, 2 * LRU_BLOCK), lambda i, j: (0, 0, 0)),
            pl.BlockSpec((1, D_MODEL), const2),
            pl.BlockSpec((1, D_MODEL), const2),
            pl.BlockSpec((1, D_MODEL), const2),
        ],
        out_specs=[
            pl.BlockSpec((bb, tt, D_MODEL), lambda i, j: (i, j, 0)),
            pl.BlockSpec((bb, 1, D_MODEL), lambda i, j: (i, 0, 0)),
            pl.BlockSpec((bb, LRU_CONV_W - 1, D_MODEL), lambda i, j: (i, 0, 0)),
        ],
        out_shape=[
            jax.ShapeDtypeStruct((b, t, D_MODEL), out_dtype),
            jax.ShapeDtypeStruct((b, 1, D_MODEL), F32),
            jax.ShapeDtypeStruct((b, LRU_CONV_W - 1, D_MODEL), F32),
        ],
        scratch_shapes=[
            pltpu.VMEM((bb, tt + SUBLANE, D_MODEL), F32),
            pltpu.VMEM((bb, tt, D_MODEL), F32),
            pltpu.VMEM((bb, tt, D_MODEL), F32),
            pltpu.VMEM((bb, 1, D_MODEL), F32),
        ],
        compiler_params=_params(("parallel", "arbitrary")),
        name="rg_lru",
    )(zf3, zf3, s0, h0, wc, bc, wrg, ba, bi, lam)


def _attn_prompt_kernel(q_ref, k_ref, v_ref, lq1_ref, lk1_ref, lq2_ref, lk2_ref, gs_ref,
                        o_ref, qq_ref, m_ref, l_ref, acc_ref, *, tq, hp, lam_init):
    qi = pl.program_id(2)
    for hh in range(hp):
        q = q_ref[:, hh * V_DIM:(hh + 1) * V_DIM]
        lane = lax.broadcasted_iota(jnp.int32, q.shape, 1)
        zero = jnp.zeros_like(q)
        qq_ref[hh, :tq, :] = jnp.where(lane < HEAD_DIM, q, zero)
        qq_ref[hh, tq:, :] = jnp.where(lane >= HEAD_DIM, q, zero)
    m_ref[...] = jnp.full(m_ref.shape, NEG_INF, F32)
    l_ref[...] = jnp.zeros(l_ref.shape, F32)
    acc_ref[...] = jnp.zeros(acc_ref.shape, F32)

    def step(ki, diagonal):
        off = pl.multiple_of(ki * tq, tq)
        chunks = [(hh, c0) for hh in range(hp) for c0 in range(0, 2 * tq, ATTN_CW)]

        def n_keys(c0):
            return min(tq, c0 % tq + ATTN_CW) if diagonal else tq

        def scores(hh, c0):
            kt = k_ref[pl.ds(off, n_keys(c0)), hh * V_DIM:(hh + 1) * V_DIM]
            s = _dot_nt(kt, qq_ref[hh, c0:c0 + ATTN_CW, :])
            if diagonal:
                key = lax.broadcasted_iota(jnp.int32, s.shape, 0)
                qry = lax.broadcasted_iota(jnp.int32, s.shape, 1) + c0 % tq
                s = jnp.where(key <= qry, s, NEG_INF)
            return s

        def softmax(hh, c0, s):
            cols = slice(c0, c0 + ATTN_CW)
            m_old = m_ref[hh, :, cols]
            m_new = jnp.maximum(m_old, jnp.max(s, axis=0, keepdims=True))
            alpha = jnp.exp2(m_old - m_new)
            p = jnp.exp2(s - m_new)
            l_ref[hh, :, cols] = alpha * l_ref[hh, :, cols] + jnp.sum(p, axis=0, keepdims=True)
            m_ref[hh, :, cols] = m_new
            return alpha, p.astype(BF16)

        def values(hh, c0, alpha, p):
            cols = slice(c0, c0 + ATTN_CW)
            vt = v_ref[pl.ds(off, n_keys(c0)), hh * V_DIM:(hh + 1) * V_DIM]
            acc_ref[hh, :, cols] = alpha * acc_ref[hh, :, cols] + _dot_tn(vt, p)

        n = len(chunks)
        s_q = {0: scores(*chunks[0])}
        if n > 1:
            s_q[1] = scores(*chunks[1])
        p_q = {0: softmax(*chunks[0], s_q.pop(0))}
        for i in range(n):
            if i + 2 < n:
                s_q[i + 2] = scores(*chunks[i + 2])
            if i + 1 < n:
                p_q[i + 1] = softmax(*chunks[i + 1], s_q.pop(i + 1))
            values(*chunks[i], *p_q.pop(i))

    def body(ki, c):
        step(ki, False)
        return c

    lax.fori_loop(0, qi, body, 0)
    step(qi, True)

    lam = _lam(lq1_ref[...], lk1_ref[...], lq2_ref[...], lk2_ref[...], lam_init)
    for hh in range(hp):
        o_t = acc_ref[hh] * (1.0 / l_ref[hh])
        o_t = o_t[:, :tq] - lam * o_t[:, tq:]
        o = o_t.T
        o_ref[:, hh * V_DIM:(hh + 1) * V_DIM] = (
            (_rms(o) * gs_ref[...]) * (1.0 - lam_init)).astype(o_ref.dtype)


def _attn_prompt(zb, lq1, lk1, lq2, lk2, gs, b, t, tq, hp, lam_init):
    m = zb.shape[0]
    nq = t // tq
    ng = N_HEADS // hp
    w = hp * V_DIM
    kern = functools.partial(_attn_prompt_kernel, tq=tq, hp=hp, lam_init=lam_init)
    vec = lambda i, h, j: (0, 0)
    return pl.pallas_call(
        kern,
        grid=(b, ng, nq),
        in_specs=[
            pl.BlockSpec((tq, w), lambda i, h, j: (i * nq + j, ZB_Q * ng + h)),
            pl.BlockSpec((t, w), lambda i, h, j: (i, ZB_K * ng + h), pipeline_mode=pl.Buffered(1)),
            pl.BlockSpec((t, w), lambda i, h, j: (i, ZB_V * ng + h), pipeline_mode=pl.Buffered(1)),
            pl.BlockSpec((1, HEAD_DIM), vec),
            pl.BlockSpec((1, HEAD_DIM), vec),
            pl.BlockSpec((1, HEAD_DIM), vec),
            pl.BlockSpec((1, HEAD_DIM), vec),
            pl.BlockSpec((1, V_DIM), vec),
        ],
        out_specs=pl.BlockSpec((tq, w), lambda i, h, j: (i * nq + j, h)),
        out_shape=jax.ShapeDtypeStruct((m, N_HEADS * V_DIM), BF16),
        scratch_shapes=[
            pltpu.VMEM((hp, 2 * tq, V_DIM), BF16),
            pltpu.VMEM((hp, 1, 2 * tq), F32),
            pltpu.VMEM((hp, 1, 2 * tq), F32),
            pltpu.VMEM((hp, V_DIM, 2 * tq), F32),
        ],
        compiler_params=_params(("parallel", "parallel", "arbitrary")),
        name="diff_attn_prompt",
    )(zb, zb, zb, lq1, lk1, lq2, lk2, gs)


def _attn_sample_kernel(pt_ref, q_ref, kn_ref, vn_ref, *rest, n_pages, tn, lam_init):
    del pt_ref
    kp_refs = rest[:n_pages]
    vp_refs = rest[n_pages:2 * n_pages]
    lq1_ref, lk1_ref, lq2_ref, lk2_ref, gs_ref, o_ref, s_ref = rest[2 * n_pages:]
    nrow = 2 * N_HEADS * tn
    n_past = n_pages * PAGE_SIZE
    HPB = 2
    nblk = N_HEADS // HPB
    rb = nrow // nblk
    fb = HPB * V_DIM

    q = q_ref[0]
    qt = jnp.concatenate([q] * (nrow // tn), axis=0)
    rj = lax.broadcasted_iota(jnp.int32, qt.shape, 0)
    cf = lax.broadcasted_iota(jnp.int32, qt.shape, 1)
    pair_j = rj >> (tn.bit_length() - 1)
    feat_blk = cf >> (HEAD_DIM.bit_length() - 1)
    qbd = jnp.where(feat_blk == pair_j, qt, 0.0)
    qblk = [qbd[g * rb:(g + 1) * rb, g * fb:(g + 1) * fb] for g in range(nblk)]

    for p in range(n_pages):
        for g in range(nblk):
            s_ref[g * rb:(g + 1) * rb, p * PAGE_SIZE:(p + 1) * PAGE_SIZE] = _dot(
                qblk[g], kp_refs[p][0, g * fb:(g + 1) * fb, :])
    s_new = _dot_nt(qbd, kn_ref[0])
    qry = lax.broadcasted_iota(jnp.int32, s_new.shape, 0) & (tn - 1)
    key = lax.broadcasted_iota(jnp.int32, s_new.shape, 1)
    s_new = jnp.where(key <= qry, s_new, NEG_INF)

    s = s_ref[...]
    mx = jnp.maximum(jnp.max(s, axis=-1, keepdims=True), jnp.max(s_new, axis=-1, keepdims=True))
    e = jnp.exp2(s - mx)
    e_new = jnp.exp2(s_new - mx)
    inv_l = 1.0 / (jnp.sum(e, axis=-1, keepdims=True) + jnp.sum(e_new, axis=-1, keepdims=True))
    s_ref[...] = e * inv_l
    p_new = e_new * inv_l

    lam = _lam(lq1_ref[...], lk1_ref[...], lq2_ref[...], lk2_ref[...], lam_init)
    outs = []
    for g in range(nblk):
        rows = slice(g * rb, (g + 1) * rb)
        acc = _dot(p_new[rows, :], vn_ref[0, :, g * fb:(g + 1) * fb])
        for p in range(n_pages):
            vg = jnp.concatenate(
                [vp_refs[p][0, pl.ds(g * HPB + hl, PAGE_SIZE, stride=N_HEADS), :]
                 for hl in range(HPB)], axis=-1)
            acc = acc + _dot(s_ref[rows, p * PAGE_SIZE:(p + 1) * PAGE_SIZE], vg)
        for hl in range(HPB):
            r0 = hl * 2 * tn
            cs = slice(hl * V_DIM, (hl + 1) * V_DIM)
            o = acc[r0:r0 + tn, cs] - lam * acc[r0 + tn:r0 + 2 * tn, cs]
            outs.append((_rms(o) * gs_ref[...]) * (1.0 - lam_init))
    o_ref[0] = jnp.concatenate(outs, axis=-1).astype(o_ref.dtype)


def _attn_sample(page_table, zb3, k3, v3, cache_kt, cache_v2, lq1, lk1, lq2, lk2, gs, lam_init):
    bd, tn, _ = k3.shape
    n_pages = page_table.shape[1]
    kern = functools.partial(_attn_sample_kernel, n_pages=n_pages, tn=tn, lam_init=lam_init)
    vec = lambda i, pt: (0, 0)

    def page_spec(p):
        return pl.BlockSpec((1, D_MODEL, PAGE_SIZE), lambda i, pt: (pt[i, p], 0, 0))

    grid_spec = pltpu.PrefetchScalarGridSpec(
        num_scalar_prefetch=1,
        grid=(bd,),
        in_specs=[
            pl.BlockSpec((1, tn, D_MODEL), lambda i, pt: (i, 0, ZB_Q)),
            pl.BlockSpec((1, tn, D_MODEL), lambda i, pt: (i, 0, 0)),
            pl.BlockSpec((1, tn, D_MODEL), lambda i, pt: (i, 0, 0)),
            *[page_spec(p) for p in range(n_pages)],
            *[page_spec(p) for p in range(n_pages)],
            pl.BlockSpec((1, HEAD_DIM), vec),
            pl.BlockSpec((1, HEAD_DIM), vec),
            pl.BlockSpec((1, HEAD_DIM), vec),
            pl.BlockSpec((1, HEAD_DIM), vec),
            pl.BlockSpec((1, V_DIM), vec),
        ],
        out_specs=pl.BlockSpec((1, tn, D_MODEL), lambda i, pt: (i, 0, 0)),
        scratch_shapes=[pltpu.VMEM((2 * N_HEADS * tn, n_pages * PAGE_SIZE), F32)],
    )
    return pl.pallas_call(
        kern,
        grid_spec=grid_spec,
        out_shape=jax.ShapeDtypeStruct((bd, tn, D_MODEL), F32),
        compiler_params=_params(("arbitrary",)),
        name="diff_attn_sample",
    )(page_table, zb3, k3, v3, *([cache_kt] * n_pages), *([cache_v2] * n_pages),
      lq1, lk1, lq2, lk2, gs)


def _mem_kv_kernel(x_ref, g_ref, w_ref, o_ref):
    h = (_rms(x_ref[...]) * g_ref[...]).astype(BF16)
    o_ref[0] = _dot(h, w_ref[...])


def _mem_kv(mem2, g, w):
    m = mem2.shape[0]
    n = w.shape[1]
    return pl.pallas_call(
        _mem_kv_kernel,
        grid=(n // D_MODEL,),
        in_specs=[
            pl.BlockSpec((m, D_MODEL), lambda j: (0, 0)),
            pl.BlockSpec((1, D_MODEL), lambda j: (0, 0)),
            pl.BlockSpec((D_MODEL, D_MODEL), lambda j: (0, j)),
        ],
        out_specs=pl.BlockSpec((1, m, D_MODEL), lambda j: (j, 0, 0)),
        out_shape=jax.ShapeDtypeStruct((n // D_MODEL, m, D_MODEL), F32),
        compiler_params=_params(("parallel",)),
        name="mem_kv",
    )(mem2, g, w)


MEM_ROW_GROUP = MEM_HEADS * (MEM_HEAD_DIM // LANE)


def _mem_attn_kernel(q_ref, mk_ref, mv_ref, o_ref, *, native):
    mxu_dtype = q_ref.dtype
    q = q_ref[0]

    def head(ref, h):
        if native:
            return jnp.concatenate(
                [ref[0, pl.ds(dc * MEM_HEADS + h, N_MEM, stride=MEM_ROW_GROUP), :]
                 for dc in range(MEM_HEAD_DIM // LANE)], axis=-1)
        return ref[0, :, h * MEM_HEAD_DIM:(h + 1) * MEM_HEAD_DIM].astype(mxu_dtype)

    outs = []
    for h in range(MEM_HEADS):
        sl = slice(h * MEM_HEAD_DIM, (h + 1) * MEM_HEAD_DIM)
        s = _dot_nt(q[:, sl], head(mk_ref, h))
        e = jnp.exp(s - jnp.max(s, axis=-1, keepdims=True))
        p = e / jnp.sum(e, axis=-1, keepdims=True)
        outs.append(_dot(p.astype(mxu_dtype), head(mv_ref, h)))
    o_ref[0] = jnp.concatenate(outs, axis=-1).astype(o_ref.dtype)


def _mem_attn(zb3, mk3, mv3, out_dtype, tq, native):
    b, t, _ = zb3.shape
    mem_block = (1,) + mk3.shape[1:]
    return pl.pallas_call(
        functools.partial(_mem_attn_kernel, native=native),
        grid=(b, t // tq),
        in_specs=[
            pl.BlockSpec((1, tq, D_MODEL), lambda i, j: (i, j, ZB_CQ)),
            pl.BlockSpec(mem_block, lambda i, j: (i, 0, 0)),
            pl.BlockSpec(mem_block, lambda i, j: (i, 0, 0)),
        ],
        out_specs=pl.BlockSpec((1, tq, D_MODEL), lambda i, j: (i, j, 0)),
        out_shape=jax.ShapeDtypeStruct((b, t, D_MODEL), out_dtype),
        compiler_params=_params(("parallel", "arbitrary")),
        name="mem_attn",
    )(zb3, mk3, mv3)


def _merge_kernel(bl_ref, ba_ref, bm_ref, g0_ref, g1_ref, g2_ref, x_ref,
                  w0_ref, w1_ref, w2_ref, wo_ref, gpost_ref, gpre_ref, x1_ref, h2_ref):
    m = jax.nn.sigmoid(g0_ref[...]) * _dot(bl_ref[...].astype(BF16), w0_ref[...])
    m = m + jax.nn.sigmoid(g1_ref[...]) * _dot(ba_ref[...].astype(BF16), w1_ref[...])
    m = m + jax.nn.sigmoid(g2_ref[...]) * _dot(bm_ref[...].astype(BF16), w2_ref[...])
    y = _dot(m.astype(BF16), wo_ref[...])
    x1 = x_ref[...] + _rms(y) * gpost_ref[...]
    x1_ref[...] = x1
    h2_ref[...] = (_rms(x1) * gpre_ref[...]).astype(h2_ref.dtype)


def _merge(br_lru, br_attn, br_mem, zf, x, w0, w1, w2, wo, gpost, gpre, h2_dtype, tm):
    m = x.shape[0]
    tok = lambda i: (i, 0)
    cst = lambda i: (0, 0)
    wspec = pl.BlockSpec((D_MODEL, D_MODEL), cst)
    return pl.pallas_call(
        _merge_kernel,
        grid=(m // tm,),
        in_specs=[
            pl.BlockSpec((tm, D_MODEL), tok),
            pl.BlockSpec((tm, D_MODEL), tok),
            pl.BlockSpec((tm, D_MODEL), tok),
            pl.BlockSpec((tm, D_MODEL), lambda i: (i, 2)),
            pl.BlockSpec((tm, D_MODEL), lambda i: (i, 3)),
            pl.BlockSpec((tm, D_MODEL), lambda i: (i, 4)),
            pl.BlockSpec((tm, D_MODEL), tok),
            wspec, wspec, wspec, wspec,
            pl.BlockSpec((1, D_MODEL), cst),
            pl.BlockSpec((1, D_MODEL), cst),
        ],
        out_specs=[pl.BlockSpec((tm, D_MODEL), tok), pl.BlockSpec((tm, D_MODEL), tok)],
        out_shape=[jax.ShapeDtypeStruct((m, D_MODEL), F32),
                   jax.ShapeDtypeStruct((m, D_MODEL), h2_dtype)],
        compiler_params=_params(("parallel",)),
        name="merge_out_proj",
    )(br_lru, br_attn, br_mem, zf, zf, zf, x, w0, w1, w2, wo, gpost, gpre)


FFN_CHUNK = 512


def _ffn_kernel(h_ref, x_ref, s0_ref, wup_ref, wc_ref, bc_ref, wdn_ref, g_ref,
                y_ref, cN_ref, up_ref, *, bb, tt):
    t = pl.program_id(1)
    W = FFN_CONV_W
    P = SUBLANE
    rows = bb * tt

    @pl.when(t == 0)
    def _():
        up_ref[:, P - (W - 1):P, :] = s0_ref[...]

    h = h_ref[...].reshape(rows, D_MODEL).astype(BF16)
    acc = jnp.zeros((rows, D_MODEL), F32)
    for c in range(D_FF // FFN_CHUNK):
        halves = []
        for base in (0, D_FF):
            cs = slice(base + c * FFN_CHUNK, base + (c + 1) * FFN_CHUNK)
            up_ref[:, P:, cs] = _dot(h, wup_ref[:, cs]).reshape(bb, tt, FFN_CHUNK)
            uc = up_ref[:, P - 2:P - 2 + tt, cs] * wc_ref[0:1, cs]
            for j in range(1, W):
                uc = uc + up_ref[:, P - 2 + j:P - 2 + j + tt, cs] * wc_ref[j:j + 1, cs]
            halves.append((uc + bc_ref[:, cs]).reshape(rows, FFN_CHUNK))
        act = (jax.nn.gelu(halves[0]) * halves[1]).astype(BF16)
        acc = acc + _dot(act, wdn_ref[c * FFN_CHUNK:(c + 1) * FFN_CHUNK, :])

    tail = up_ref[:, tt + P - (W - 1):tt + P, :]
    up_ref[:, P - (W - 1):P, :] = tail
    cN_ref[...] = tail
    y = x_ref[...].reshape(rows, D_MODEL) + _rms(acc) * g_ref[...]
    y_ref[...] = y.reshape(bb, tt, D_MODEL)


def _ffn(h3, x3, s0, wup, wc, bc, wdn, g, bb, tt):
    b, t, _ = x3.shape
    kern = functools.partial(_ffn_kernel, bb=bb, tt=tt)
    cst = lambda i, j: (0, 0)
    once = pl.Buffered(1)
    return pl.pallas_call(
        kern,
        grid=(b // bb, t // tt),
        in_specs=[
            pl.BlockSpec((bb, tt, D_MODEL), lambda i, j: (i, j, 0)),
            pl.BlockSpec((bb, tt, D_MODEL), lambda i, j: (i, j, 0)),
            pl.BlockSpec((bb, FFN_CONV_W - 1, 2 * D_FF), lambda i, j: (i, 0, 0)),
            pl.BlockSpec((D_MODEL, 2 * D_FF), cst, pipeline_mode=once),
            pl.BlockSpec((FFN_CONV_W, 2 * D_FF), cst),
            pl.BlockSpec((1, 2 * D_FF), cst),
            pl.BlockSpec((D_FF, D_MODEL), cst, pipeline_mode=once),
            pl.BlockSpec((1, D_MODEL), cst),
        ],
        out_specs=[
            pl.BlockSpec((bb, tt, D_MODEL), lambda i, j: (i, j, 0)),
            pl.BlockSpec((bb, FFN_CONV_W - 1, 2 * D_FF), lambda i, j: (i, 0, 0)),
        ],
        out_shape=[
            jax.ShapeDtypeStruct((b, t, D_MODEL), F32),
            jax.ShapeDtypeStruct((b, FFN_CONV_W - 1, 2 * D_FF), F32),
        ],
        scratch_shapes=[pltpu.VMEM((bb, tt + SUBLANE, 2 * D_FF), F32)],
        compiler_params=_params(("parallel", "arbitrary")),
        name="conv_ffn",
    )(h3, x3, s0, wup, wc, bc, wdn, g)


def _layer(x3, attn_fn, mk3, mv3, lru_h0, lru_conv0, ffn_conv0, wts, *, prompt):
    b, t, _ = x3.shape
    m = b * t
    x2 = x3.reshape(m, D_MODEL)
    act_dtype = BF16 if prompt else F32
    tm = 512
    zf, zb, k, v = _in_proj(x2, wts["g_pre_mix"], wts["w_in"], act_dtype, tm)

    bb, tt = (1, 512) if prompt else (32, t)
    br_lru, lru_h, lru_conv = _lru(
        zf.reshape(b, t, -1), lru_conv0, lru_h0.reshape(b, 1, D_MODEL),
        wts["w_lru_conv"], wts["b_lru_conv"], wts["w_rg"], wts["b_rg_a"], wts["b_rg_i"],
        wts["lru_lambda"], act_dtype, bb, tt)

    br_attn = attn_fn(zb, k, v)
    br_mem = _mem_attn(zb.reshape(b, t, -1), mk3, mv3, act_dtype, 512 if prompt else t,
                        native=not prompt)

    x1, h2 = _merge(br_lru.reshape(m, D_MODEL), br_attn.reshape(m, D_MODEL),
                    br_mem.reshape(m, D_MODEL), zf, x2,
                    wts["w_br_lru"], wts["w_br_attn"], wts["w_br_mem"], wts["w_out"],
                    wts["g_post_mix"], wts["g_pre_ffn"], act_dtype, tm)

    bb, tt = (1, 256) if prompt else (16, t)
    y, ffn_conv = _ffn(h2.reshape(b, t, D_MODEL), x1.reshape(b, t, D_MODEL), ffn_conv0,
                       wts["w_up"], wts["w_ffn_conv"], wts["b_ffn_conv"], wts["w_down"],
                       wts["g_post_ffn"], bb, tt)
    return y, k, v, lru_h.reshape(b, D_MODEL), lru_conv, ffn_conv


def _lambda_init(layer_idx):
    return 0.8 - 0.6 * math.exp(-0.3 * layer_idx)


def kernel(x_prompt, x_sample, mem_prompt, cache_k, cache_v, page_table, cache_mem_k, cache_mem_v, state_lru_h, state_lru_conv, state_ffn_conv, g_pre_mix, w_in, w_lru_conv, b_lru_conv, w_rg_a, b_rg_a, w_rg_i, b_rg_i, lru_lambda, lambda_q1, lambda_k1, lambda_q2, lambda_k2, g_subln, g_mem, w_mem_kv, w_br_lru, w_br_attn, w_br_mem, w_out, g_post_mix, g_pre_ffn, w_up, w_ffn_conv, b_ffn_conv, w_down, g_post_ffn):
    depth = w_in.shape[0]
    bp, tp, _ = x_prompt.shape
    bd, td, _ = x_sample.shape
    xp, xs = x_prompt, x_sample
    outs = [[] for _ in range(12)]
    row = lambda a: a.reshape(1, -1)

    for l in range(depth):
        lam_init = _lambda_init(l)
        wi = w_in[l]
        blk = lambda i: wi[:, i * D_MODEL:(i + 1) * D_MODEL]
        w_in_p = jnp.concatenate(
            [blk(0), blk(1), blk(6), blk(7), blk(8),
             blk(2) * (LOG2_E / math.sqrt(HEAD_DIM)), blk(3), blk(4),
             blk(5) * (1.0 / math.sqrt(MEM_HEAD_DIM))], axis=1).astype(BF16)
        wts = dict(
            g_pre_mix=row(g_pre_mix[l]), w_in=w_in_p,
            w_lru_conv=w_lru_conv[l], b_lru_conv=row(b_lru_conv[l]),
            w_rg=jnp.concatenate([w_rg_a[l], w_rg_i[l]], axis=-1).astype(BF16),
            b_rg_a=row(b_rg_a[l]), b_rg_i=row(b_rg_i[l]), lru_lambda=row(lru_lambda[l]),
            w_br_lru=w_br_lru[l].astype(BF16), w_br_attn=w_br_attn[l].astype(BF16),
            w_br_mem=w_br_mem[l].astype(BF16), w_out=w_out[l].astype(BF16),
            g_post_mix=row(g_post_mix[l]), g_pre_ffn=row(g_pre_ffn[l]),
            w_up=w_up[l].astype(BF16), w_ffn_conv=w_ffn_conv[l], b_ffn_conv=row(b_ffn_conv[l]),
            w_down=w_down[l].astype(BF16), g_post_ffn=row(g_post_ffn[l]),
        )
        lams = (row(lambda_q1[l]), row(lambda_k1[l]), row(lambda_q2[l]), row(lambda_k2[l]))
        gs = row(g_subln[l])

        mkv = _mem_kv(mem_prompt.reshape(bp * N_MEM, D_MODEL), row(g_mem[l]),
                      w_mem_kv[l].astype(BF16))
        mk_p = mkv[0].reshape(bp, N_MEM, D_MODEL)
        mv_p = mkv[1].reshape(bp, N_MEM, D_MODEL)
        attn_p = lambda zb, k, v: _attn_prompt(zb, *lams, gs, bp, tp, ATTN_TQ, ATTN_HP, lam_init)
        xp, k_p, v_p, h_p, c_p, f_p = _layer(
            xp, attn_p, mk_p, mv_p,
            jnp.zeros((bp, D_MODEL), F32),
            jnp.zeros((bp, LRU_CONV_W - 1, D_MODEL), F32),
            jnp.zeros((bp, FFN_CONV_W - 1, 2 * D_FF), F32),
            wts, prompt=True)

        ck = cache_k[l].transpose(0, 2, 3, 4, 1).reshape(-1, D_MODEL, PAGE_SIZE)
        cv = cache_v[l].reshape(-1, PAGE_SIZE * N_HEADS, V_DIM)
        mem_view = lambda c: c.reshape(bd, N_MEM, MEM_HEADS, MEM_HEAD_DIM // LANE, LANE).transpose(
            0, 1, 3, 2, 4).reshape(bd, N_MEM * MEM_ROW_GROUP, LANE)
        attn_s = lambda zb, k, v: _attn_sample(
            page_table, zb.reshape(bd, td, -1), k.reshape(bd, td, D_MODEL),
            v.reshape(bd, td, D_MODEL), ck, cv, *lams, gs, lam_init)
        xs, k_s, v_s, h_s, c_s, f_s = _layer(
            xs, attn_s, mem_view(cache_mem_k[l]), mem_view(cache_mem_v[l]),
            state_lru_h[l], state_lru_conv[l], state_ffn_conv[l], wts, prompt=False)

        vals = (k_p.reshape(bp, tp, N_HEADS, 2, HEAD_DIM), v_p.reshape(bp, tp, N_HEADS, V_DIM),
                mk_p.reshape(bp, N_MEM, MEM_HEADS, MEM_HEAD_DIM),
                mv_p.reshape(bp, N_MEM, MEM_HEADS, MEM_HEAD_DIM), h_p, c_p, f_p,
                k_s.reshape(bd, td, N_HEADS, 2, HEAD_DIM), v_s.reshape(bd, td, N_HEADS, V_DIM),
                h_s, c_s, f_s)
        for o, val in zip(outs, vals):
            o.append(val)

    return (xp, xs, *[jnp.stack(o) for o in outs])
```

```python
import functools
import math

import jax
import jax.numpy as jnp
from jax import lax
from jax.experimental import pallas as pl
from jax.experimental.pallas import tpu as pltpu

F32 = jnp.float32
BF16 = jnp.bfloat16

D_MODEL = 1024
N_HEADS = 8
HEAD_DIM = 64
V_DIM = 128
N_MEM = 256
MEM_HEADS = 4
MEM_HEAD_DIM = 256
N_LRU_BLOCKS = 8
LRU_BLOCK = 128
LRU_CONV_W = 4
LRU_C = 8.0
D_FF = 3072
FFN_CONV_W = 3
RMS_EPS = 1e-6
NEG_INF = -1e30
LOG2_E = math.log2(math.e)
PAGE_SIZE = 128

N_F32_BLOCKS = 5
N_IN_BLOCKS = 9
ZB_Q, ZB_K, ZB_V, ZB_CQ = 0, 1, 2, 3

LANE = 128
SUBLANE = 8
VMEM_LIMIT = 56 * 1024 * 1024
ATTN_TQ = 512
ATTN_HP = 4
ATTN_CW = 256


def _params(sem, vmem=VMEM_LIMIT):
    return pltpu.CompilerParams(dimension_semantics=sem, vmem_limit_bytes=vmem)


def _rms(x):
    return x * lax.rsqrt(jnp.mean(x * x, axis=-1, keepdims=True) + RMS_EPS)


def _dot(a, b):
    return jnp.dot(a, b, preferred_element_type=F32)


def _dot_nt(a, b):
    return lax.dot_general(a, b, (((1,), (1,)), ((), ())), preferred_element_type=F32)


def _dot_tn(a, b):
    return lax.dot_general(a, b, (((0,), (0,)), ((), ())), preferred_element_type=F32)


def _lam(q1, k1, q2, k2, lam_init):
    s1 = jnp.sum(q1 * k1, axis=-1, keepdims=True)
    s2 = jnp.sum(q2 * k2, axis=-1, keepdims=True)
    return jnp.exp(s1) - jnp.exp(s2) + lam_init


def _in_proj_kernel(x_ref, g_ref, w_ref, wkt_ref, zf_ref, zb_ref, k_ref, v_ref, h_ref, *,
                    k_transposed):
    n = pl.program_id(1)

    @pl.when(n == 0)
    def _():
        h_ref[...] = (_rms(x_ref[...]) * g_ref[...]).astype(BF16)

    @pl.when(n < N_F32_BLOCKS)
    def _():
        zf_ref[...] = _dot(h_ref[...], w_ref[...])

    @pl.when((n == N_F32_BLOCKS + ZB_Q) | (n == N_F32_BLOCKS + ZB_CQ))
    def _():
        zb_ref[...] = _dot(h_ref[...], w_ref[...]).astype(zb_ref.dtype)

    @pl.when(n == N_F32_BLOCKS + ZB_K)
    def _():
        z = _dot(h_ref[...], w_ref[...])
        zb_ref[...] = z.astype(zb_ref.dtype)
        if k_transposed:
            k_ref[0] = _dot_nt(wkt_ref[...], h_ref[...])
        else:
            k_ref[...] = z

    @pl.when(n == N_F32_BLOCKS + ZB_V)
    def _():
        z = _dot(h_ref[...], w_ref[...])
        zb_ref[...] = z.astype(zb_ref.dtype)
        v_ref[...] = z


def _in_proj(x, g, w_in, w_kt, zb_dtype, tm, seq_len, k_transposed):
    m = x.shape[0]
    nb = N_F32_BLOCKS
    nt = seq_len // tm
    if k_transposed:
        k_spec = pl.BlockSpec((1, D_MODEL, tm), lambda i, n: (i // nt, 0, i % nt))
        k_shape = jax.ShapeDtypeStruct((m // seq_len, D_MODEL, seq_len), F32)
    else:
        k_spec = pl.BlockSpec((tm, D_MODEL), lambda i, n: (i, 0))
        k_shape = jax.ShapeDtypeStruct((m, D_MODEL), F32)
    return pl.pallas_call(
        functools.partial(_in_proj_kernel, k_transposed=k_transposed),
        grid=(m // tm, N_IN_BLOCKS),
        in_specs=[
            pl.BlockSpec((tm, D_MODEL), lambda i, n: (i, 0)),
            pl.BlockSpec((1, D_MODEL), lambda i, n: (0, 0)),
            pl.BlockSpec((D_MODEL, D_MODEL), lambda i, n: (0, n)),
            pl.BlockSpec((D_MODEL, D_MODEL), lambda i, n: (0, 0), pipeline_mode=pl.Buffered(1)),
        ],
        out_specs=[
            pl.BlockSpec((tm, D_MODEL), lambda i, n: (i, jnp.minimum(n, nb - 1))),
            pl.BlockSpec((tm, D_MODEL), lambda i, n: (i, jnp.maximum(n - nb, 0))),
            k_spec,
            pl.BlockSpec((tm, D_MODEL), lambda i, n: (i, 0)),
        ],
        out_shape=[
            jax.ShapeDtypeStruct((m, nb * D_MODEL), F32),
            jax.ShapeDtypeStruct((m, (N_IN_BLOCKS - nb) * D_MODEL), zb_dtype),
            k_shape,
            jax.ShapeDtypeStruct((m, D_MODEL), F32),
        ],
        scratch_shapes=[pltpu.VMEM((tm, D_MODEL), BF16)],
        compiler_params=_params(("parallel", "arbitrary")),
        name="in_proj",
    )(x, g, w_in, w_kt)


def _lru_kernel(lx_ref, ly_ref, s0_ref, h0_ref, wc_ref, bc_ref, wrg_ref, ba_ref, bi_ref,
                lam_ref, out_ref, hN_ref, cN_ref, xp_ref, a_ref, u_ref, hc_ref, *, bb, tt):
    t = pl.program_id(1)
    W = LRU_CONV_W
    P = SUBLANE

    @pl.when(t == 0)
    def _():
        xp_ref[:, P - (W - 1):P, :] = s0_ref[...]
        hc_ref[...] = h0_ref[...]

    xp_ref[:, P:, :] = lx_ref[...]
    xc = xp_ref[:, P - 3:P - 3 + tt, :] * wc_ref[0:1, :]
    for j in range(1, W):
        xc = xc + xp_ref[:, P - 3 + j:P - 3 + j + tt, :] * wc_ref[j:j + 1, :]
    xc = xc + bc_ref[...]
    tail = xp_ref[:, tt + P - (W - 1):tt + P, :]
    xp_ref[:, P - (W - 1):P, :] = tail
    cN_ref[...] = tail

    x2 = xc.reshape(bb * tt, D_MODEL)
    xb = x2.astype(BF16)
    lin = [_dot(xb[:, n * LRU_BLOCK:(n + 1) * LRU_BLOCK], wrg_ref[n])
           for n in range(N_LRU_BLOCKS)]
    r_lin = jnp.concatenate([z[:, :LRU_BLOCK] for z in lin], axis=-1)
    i_lin = jnp.concatenate([z[:, LRU_BLOCK:] for z in lin], axis=-1)
    r = jax.nn.sigmoid(r_lin + ba_ref[...])
    gi = jax.nn.sigmoid(i_lin + bi_ref[...])
    neg_lam = -lam_ref[...]
    softplus = jnp.maximum(neg_lam, 0.0) + jnp.log1p(jnp.exp(-jnp.abs(neg_lam)))
    log_a = (-LRU_C * softplus) * r
    a = jnp.exp(log_a)
    u = jnp.sqrt(1.0 - a * a) * (gi * x2)

    g8 = (bb * tt) // SUBLANE
    a3 = a.reshape(g8, SUBLANE, D_MODEL)
    u3 = u.reshape(g8, SUBLANE, D_MODEL)
    row = lax.broadcasted_iota(jnp.int32, (g8, SUBLANE, D_MODEL), 1)
    for s in (1, 2, 4):
        a_sh = pltpu.roll(a3, s, 1)
        u_sh = pltpu.roll(u3, s, 1)
        ok = row >= s
        u3 = jnp.where(ok, a3 * u_sh + u3, u3)
        a3 = jnp.where(ok, a3 * a_sh, a3)
    a_ref[...] = a3.reshape(bb, tt, D_MODEL)
    u_ref[...] = u3.reshape(bb, tt, D_MODEL)

    def body(g, h):
        off = pl.multiple_of(g * SUBLANE, SUBLANE)
        hg = a_ref[:, pl.ds(off, SUBLANE), :] * h + u_ref[:, pl.ds(off, SUBLANE), :]
        u_ref[:, pl.ds(off, SUBLANE), :] = hg
        return hg[:, SUBLANE - 1:SUBLANE, :]

    h_last = lax.fori_loop(0, tt // SUBLANE, body, hc_ref[...])
    hc_ref[...] = h_last
    hN_ref[...] = h_last
    out_ref[...] = (jax.nn.gelu(ly_ref[...]) * u_ref[...]).astype(out_ref.dtype)


def _lru(zf3, s0, h0, wc, bc, wrg, ba, bi, lam, out_dtype, bb, tt):
    b, t, _ = zf3.shape
    kern = functools.partial(_lru_kernel, bb=bb, tt=tt)
    const2 = lambda i, j: (0, 0)
    return pl.pallas_call(
        kern,
        grid=(b // bb, t // tt),
        in_specs=[
            pl.BlockSpec((bb, tt, D_MODEL), lambda i, j: (i, j, 0)),
            pl.BlockSpec((bb, tt, D_MODEL), lambda i, j: (i, j, 1)),
            pl.BlockSpec((bb, LRU_CONV_W - 1, D_MODEL), lambda i, j: (i, 0, 0)),
            pl.BlockSpec((bb, 1, D_MODEL), lambda i, j: (i, 0, 0)),
            pl.BlockSpec((LRU_CONV_W, D_MODEL), const2),
            pl.BlockSpec((1, D_MODEL), const2),
            pl.BlockSpec((N_LRU_BLOCKS, LRU_BLOCK, 2 * LRU_BLOCK), lambda i, j: (0, 0, 0)),
            pl.BlockSpec((1, D_MODEL), const2),
            pl.BlockSpec((1, D_MODEL), const2),
            pl.BlockSpec((1, D_MODEL), const2),
        ],
        out_specs=[
            pl.BlockSpec((bb, tt, D_MODEL), lambda i, j: (i, j, 0)),
            pl.BlockSpec((bb, 1, D_MODEL), lambda i, j: (i, 0, 0)),
            pl.BlockSpec((bb, LRU_CONV_W - 1, D_MODEL), lambda i, j: (i, 0, 0)),
        ],
        out_shape=[
            jax.ShapeDtypeStruct((b, t, D_MODEL), out_dtype),
            jax.ShapeDtypeStruct((b, 1, D_MODEL), F32),
            jax.ShapeDtypeStruct((b, LRU_CONV_W - 1, D_MODEL), F32),
        ],
        scratch_shapes=[
            pltpu.VMEM((bb, tt + SUBLANE, D_MODEL), F32),
            pltpu.VMEM((bb, tt, D_MODEL), F32),
            pltpu.VMEM((bb, tt, D_MODEL), F32),
            pltpu.VMEM((bb, 1, D_MODEL), F32),
        ],
        compiler_params=_params(("parallel", "arbitrary")),
        name="rg_lru",
    )(zf3, zf3, s0, h0, wc, bc, wrg, ba, bi, lam)


def _attn_prompt_kernel(q_ref, k_ref, v_ref, lq1_ref, lk1_ref, lq2_ref, lk2_ref, gs_ref,
                        o_ref, qq_ref, m_ref, l_ref, acc_ref, *, tq, hp, lam_init):
    qi = pl.program_id(2)
    for hh in range(hp):
        q = q_ref[:, hh * V_DIM:(hh + 1) * V_DIM]
        lane = lax.broadcasted_iota(jnp.int32, q.shape, 1)
        zero = jnp.zeros_like(q)
        qq_ref[hh, :tq, :] = jnp.where(lane < HEAD_DIM, q, zero)
        qq_ref[hh, tq:, :] = jnp.where(lane >= HEAD_DIM, q, zero)
    m_ref[...] = jnp.full(m_ref.shape, NEG_INF, F32)
    l_ref[...] = jnp.zeros(l_ref.shape, F32)
    acc_ref[...] = jnp.zeros(acc_ref.shape, F32)

    def step(ki, diagonal):
        off = pl.multiple_of(ki * tq, tq)
        chunks = [(hh, c0) for hh in range(hp) for c0 in range(0, 2 * tq, ATTN_CW)]

        def n_keys(c0):
            return min(tq, c0 % tq + ATTN_CW) if diagonal else tq

        def scores(hh, c0):
            kt = k_ref[pl.ds(off, n_keys(c0)), hh * V_DIM:(hh + 1) * V_DIM]
            s = _dot_nt(kt, qq_ref[hh, c0:c0 + ATTN_CW, :])
            if diagonal:
                key = lax.broadcasted_iota(jnp.int32, s.shape, 0)
                qry = lax.broadcasted_iota(jnp.int32, s.shape, 1) + c0 % tq
                s = jnp.where(key <= qry, s, NEG_INF)
            return s

        def softmax(hh, c0, s):
            cols = slice(c0, c0 + ATTN_CW)
            m_old = m_ref[hh, :, cols]
            m_new = jnp.maximum(m_old, jnp.max(s, axis=0, keepdims=True))
            alpha = jnp.exp2(m_old - m_new)
            p = jnp.exp2(s - m_new)
            l_ref[hh, :, cols] = alpha * l_ref[hh, :, cols] + jnp.sum(p, axis=0, keepdims=True)
            m_ref[hh, :, cols] = m_new
            return alpha, p.astype(BF16)

        def values(hh, c0, alpha, p):
            cols = slice(c0, c0 + ATTN_CW)
            vt = v_ref[pl.ds(off, n_keys(c0)), hh * V_DIM:(hh + 1) * V_DIM]
            acc_ref[hh, :, cols] = alpha * acc_ref[hh, :, cols] + _dot_tn(vt, p)

        n = len(chunks)
        s_q = {0: scores(*chunks[0])}
        if n > 1:
            s_q[1] = scores(*chunks[1])
        p_q = {0: softmax(*chunks[0], s_q.pop(0))}
        for i in range(n):
            if i + 2 < n:
                s_q[i + 2] = scores(*chunks[i + 2])
            if i + 1 < n:
                p_q[i + 1] = softmax(*chunks[i + 1], s_q.pop(i + 1))
            values(*chunks[i], *p_q.pop(i))

    def body(ki, c):
        step(ki, False)
        return c

    lax.fori_loop(0, qi, body, 0)
    step(qi, True)

    lam = _lam(lq1_ref[...], lk1_ref[...], lq2_ref[...], lk2_ref[...], lam_init)
    for hh in range(hp):
        o_t = acc_ref[hh] * (1.0 / l_ref[hh])
        o_t = o_t[:, :tq] - lam * o_t[:, tq:]
        o = o_t.T
        o_ref[:, hh * V_DIM:(hh + 1) * V_DIM] = (
            (_rms(o) * gs_ref[...]) * (1.0 - lam_init)).astype(o_ref.dtype)


def _attn_prompt(zb, lq1, lk1, lq2, lk2, gs, b, t, tq, hp, lam_init):
    m = zb.shape[0]
    nq = t // tq
    ng = N_HEADS // hp
    w = hp * V_DIM
    kern = functools.partial(_attn_prompt_kernel, tq=tq, hp=hp, lam_init=lam_init)
    vec = lambda i, h, j: (0, 0)
    return pl.pallas_call(
        kern,
        grid=(b, ng, nq),
        in_specs=[
            pl.BlockSpec((tq, w), lambda i, h, j: (i * nq + j, ZB_Q * ng + h)),
            pl.BlockSpec((t, w), lambda i, h, j: (i, ZB_K * ng + h), pipeline_mode=pl.Buffered(1)),
            pl.BlockSpec((t, w), lambda i, h, j: (i, ZB_V * ng + h), pipeline_mode=pl.Buffered(1)),
            pl.BlockSpec((1, HEAD_DIM), vec),
            pl.BlockSpec((1, HEAD_DIM), vec),
            pl.BlockSpec((1, HEAD_DIM), vec),
            pl.BlockSpec((1, HEAD_DIM), vec),
            pl.BlockSpec((1, V_DIM), vec),
        ],
        out_specs=pl.BlockSpec((tq, w), lambda i, h, j: (i * nq + j, h)),
        out_shape=jax.ShapeDtypeStruct((m, N_HEADS * V_DIM), BF16),
        scratch_shapes=[
            pltpu.VMEM((hp, 2 * tq, V_DIM), BF16),
            pltpu.VMEM((hp, 1, 2 * tq), F32),
            pltpu.VMEM((hp, 1, 2 * tq), F32),
            pltpu.VMEM((hp, V_DIM, 2 * tq), F32),
        ],
        compiler_params=_params(("parallel", "parallel", "arbitrary")),
        name="diff_attn_prompt",
    )(zb, zb, zb, lq1, lk1, lq2, lk2, gs)


def _attn_sample_kernel(pt_ref, q_ref, kn_ref, vn_ref, *rest, n_pages, tn, lam_init):
    del pt_ref
    kp_refs = rest[:n_pages]
    vp_refs = rest[n_pages:2 * n_pages]
    lq1_ref, lk1_ref, lq2_ref, lk2_ref, gs_ref, o_ref, s_ref = rest[2 * n_pages:]
    nrow = 2 * N_HEADS * tn
    n_past = n_pages * PAGE_SIZE
    HPB = 2
    nblk = N_HEADS // HPB
    rb = nrow // nblk
    fb = HPB * V_DIM

    q = q_ref[0]
    qt = jnp.concatenate([q] * (nrow // tn), axis=0)
    rj = lax.broadcasted_iota(jnp.int32, qt.shape, 0)
    cf = lax.broadcasted_iota(jnp.int32, qt.shape, 1)
    pair_j = rj >> (tn.bit_length() - 1)
    feat_blk = cf >> (HEAD_DIM.bit_length() - 1)
    qbd = jnp.where(feat_blk == pair_j, qt, 0.0)
    qblk = [qbd[g * rb:(g + 1) * rb, g * fb:(g + 1) * fb] for g in range(nblk)]

    for p in range(n_pages):
        for g in range(nblk):
            s_ref[g * rb:(g + 1) * rb, p * PAGE_SIZE:(p + 1) * PAGE_SIZE] = _dot(
                qblk[g], kp_refs[p][0, g * fb:(g + 1) * fb, :])
    s_new = _dot_nt(qbd, kn_ref[0])
    qry = lax.broadcasted_iota(jnp.int32, s_new.shape, 0) & (tn - 1)
    key = lax.broadcasted_iota(jnp.int32, s_new.shape, 1)
    s_new = jnp.where(key <= qry, s_new, NEG_INF)

    s = s_ref[...]
    mx = jnp.maximum(jnp.max(s, axis=-1, keepdims=True), jnp.max(s_new, axis=-1, keepdims=True))
    e = jnp.exp2(s - mx)
    e_new = jnp.exp2(s_new - mx)
    inv_l = 1.0 / (jnp.sum(e, axis=-1, keepdims=True) + jnp.sum(e_new, axis=-1, keepdims=True))
    s_ref[...] = e * inv_l
    p_new = e_new * inv_l

    lam = _lam(lq1_ref[...], lk1_ref[...], lq2_ref[...], lk2_ref[...], lam_init)
    outs = []
    for g in range(nblk):
        rows = slice(g * rb, (g + 1) * rb)
        acc = _dot(p_new[rows, :], vn_ref[0, :, g * fb:(g + 1) * fb])
        for p in range(n_pages):
            vg = jnp.concatenate(
                [vp_refs[p][0, pl.ds(g * HPB + hl, PAGE_SIZE, stride=N_HEADS), :]
                 for hl in range(HPB)], axis=-1)
            acc = acc + _dot(s_ref[rows, p * PAGE_SIZE:(p + 1) * PAGE_SIZE], vg)
        for hl in range(HPB):
            r0 = hl * 2 * tn
            cs = slice(hl * V_DIM, (hl + 1) * V_DIM)
            o = acc[r0:r0 + tn, cs] - lam * acc[r0 + tn:r0 + 2 * tn, cs]
            outs.append((_rms(o) * gs_ref[...]) * (1.0 - lam_init))
    o_ref[0] = jnp.concatenate(outs, axis=-1).astype(o_ref.dtype)


def _attn_sample(page_table, zb3, k3, v3, cache_kt, cache_v2, lq1, lk1, lq2, lk2, gs, lam_init):
    bd, tn, _ = k3.shape
    n_pages = page_table.shape[1]
    kern = functools.partial(_attn_sample_kernel, n_pages=n_pages, tn=tn, lam_init=lam_init)
    vec = lambda i, pt: (0, 0)

    def page_spec(p):
        return pl.BlockSpec((1, D_MODEL, PAGE_SIZE), lambda i, pt: (pt[i, p], 0, 0))

    grid_spec = pltpu.PrefetchScalarGridSpec(
        num_scalar_prefetch=1,
        grid=(bd,),
        in_specs=[
            pl.BlockSpec((1, tn, D_MODEL), lambda i, pt: (i, 0, ZB_Q)),
            pl.BlockSpec((1, tn, D_MODEL), lambda i, pt: (i, 0, 0)),
            pl.BlockSpec((1, tn, D_MODEL), lambda i, pt: (i, 0, 0)),
            *[page_spec(p) for p in range(n_pages)],
            *[page_spec(p) for p in range(n_pages)],
            pl.BlockSpec((1, HEAD_DIM), vec),
            pl.BlockSpec((1, HEAD_DIM), vec),
            pl.BlockSpec((1, HEAD_DIM), vec),
            pl.BlockSpec((1, HEAD_DIM), vec),
            pl.BlockSpec((1, V_DIM), vec),
        ],
        out_specs=pl.BlockSpec((1, tn, D_MODEL), lambda i, pt: (i, 0, 0)),
        scratch_shapes=[pltpu.VMEM((2 * N_HEADS * tn, n_pages * PAGE_SIZE), F32)],
    )
    return pl.pallas_call(
        kern,
        grid_spec=grid_spec,
        out_shape=jax.ShapeDtypeStruct((bd, tn, D_MODEL), F32),
        compiler_params=_params(("arbitrary",)),
        name="diff_attn_sample",
    )(page_table, zb3, k3, v3, *([cache_kt] * n_pages), *([cache_v2] * n_pages),
      lq1, lk1, lq2, lk2, gs)


def _mem_kv_kernel(x_ref, g_ref, w_ref, o_ref):
    h = (_rms(x_ref[...]) * g_ref[...]).astype(BF16)
    o_ref[0] = _dot(h, w_ref[...])


def _mem_kv(mem2, g, w):
    m = mem2.shape[0]
    n = w.shape[1]
    return pl.pallas_call(
        _mem_kv_kernel,
        grid=(n // D_MODEL,),
        in_specs=[
            pl.BlockSpec((m, D_MODEL), lambda j: (0, 0)),
            pl.BlockSpec((1, D_MODEL), lambda j: (0, 0)),
            pl.BlockSpec((D_MODEL, D_MODEL), lambda j: (0, j)),
        ],
        out_specs=pl.BlockSpec((1, m, D_MODEL), lambda j: (j, 0, 0)),
        out_shape=jax.ShapeDtypeStruct((n // D_MODEL, m, D_MODEL), F32),
        compiler_params=_params(("parallel",)),
        name="mem_kv",
    )(mem2, g, w)


MEM_ROW_GROUP = MEM_HEADS * (MEM_HEAD_DIM // LANE)


def _mem_attn_kernel(q_ref, mk_ref, mv_ref, o_ref, *, native):
    mxu_dtype = q_ref.dtype
    q = q_ref[0]

    def head(ref, h):
        if native:
            return jnp.concatenate(
                [ref[0, pl.ds(dc * MEM_HEADS + h, N_MEM, stride=MEM_ROW_GROUP), :]
                 for dc in range(MEM_HEAD_DIM // LANE)], axis=-1)
        return ref[0, :, h * MEM_HEAD_DIM:(h + 1) * MEM_HEAD_DIM].astype(mxu_dtype)

    scores = [_dot_nt(q[:, h * MEM_HEAD_DIM:(h + 1) * MEM_HEAD_DIM], head(mk_ref, h))
              for h in range(MEM_HEADS)]
    probs = []
    for s in scores:
        e = jnp.exp(s - jnp.max(s, axis=-1, keepdims=True))
        probs.append((e / jnp.sum(e, axis=-1, keepdims=True)).astype(mxu_dtype))
    outs = [_dot(probs[h], head(mv_ref, h)) for h in range(MEM_HEADS)]
    o_ref[0] = jnp.concatenate(outs, axis=-1).astype(o_ref.dtype)


def _mem_attn(zb3, mk3, mv3, out_dtype, tq, native):
    b, t, _ = zb3.shape
    mem_block = (1,) + mk3.shape[1:]
    return pl.pallas_call(
        functools.partial(_mem_attn_kernel, native=native),
        grid=(b, t // tq),
        in_specs=[
            pl.BlockSpec((1, tq, D_MODEL), lambda i, j: (i, j, ZB_CQ)),
            pl.BlockSpec(mem_block, lambda i, j: (i, 0, 0)),
            pl.BlockSpec(mem_block, lambda i, j: (i, 0, 0)),
        ],
        out_specs=pl.BlockSpec((1, tq, D_MODEL), lambda i, j: (i, j, 0)),
        out_shape=jax.ShapeDtypeStruct((b, t, D_MODEL), out_dtype),
        compiler_params=_params(("parallel", "arbitrary")),
        name="mem_attn",
    )(zb3, mk3, mv3)


def _merge_kernel(bl_ref, ba_ref, bm_ref, g0_ref, g1_ref, g2_ref, x_ref,
                  w0_ref, w1_ref, w2_ref, wo_ref, gpost_ref, gpre_ref, x1_ref, h2_ref):
    m = jax.nn.sigmoid(g0_ref[...]) * _dot(bl_ref[...].astype(BF16), w0_ref[...])
    m = m + jax.nn.sigmoid(g1_ref[...]) * _dot(ba_ref[...].astype(BF16), w1_ref[...])
    m = m + jax.nn.sigmoid(g2_ref[...]) * _dot(bm_ref[...].astype(BF16), w2_ref[...])
    y = _dot(m.astype(BF16), wo_ref[...])
    x1 = x_ref[...] + _rms(y) * gpost_ref[...]
    x1_ref[...] = x1
    h2_ref[...] = (_rms(x1) * gpre_ref[...]).astype(h2_ref.dtype)


def _merge(br_lru, br_attn, br_mem, zf, x, w0, w1, w2, wo, gpost, gpre, h2_dtype, tm):
    m = x.shape[0]
    tok = lambda i: (i, 0)
    cst = lambda i: (0, 0)
    wspec = pl.BlockSpec((D_MODEL, D_MODEL), cst)
    return pl.pallas_call(
        _merge_kernel,
        grid=(m // tm,),
        in_specs=[
            pl.BlockSpec((tm, D_MODEL), tok),
            pl.BlockSpec((tm, D_MODEL), tok),
            pl.BlockSpec((tm, D_MODEL), tok),
            pl.BlockSpec((tm, D_MODEL), lambda i: (i, 2)),
            pl.BlockSpec((tm, D_MODEL), lambda i: (i, 3)),
            pl.BlockSpec((tm, D_MODEL), lambda i: (i, 4)),
            pl.BlockSpec((tm, D_MODEL), tok),
            wspec, wspec, wspec, wspec,
            pl.BlockSpec((1, D_MODEL), cst),
            pl.BlockSpec((1, D_MODEL), cst),
        ],
        out_specs=[pl.BlockSpec((tm, D_MODEL), tok), pl.BlockSpec((tm, D_MODEL), tok)],
        out_shape=[jax.ShapeDtypeStruct((m, D_MODEL), F32),
                   jax.ShapeDtypeStruct((m, D_MODEL), h2_dtype)],
        compiler_params=_params(("parallel",)),
        name="merge_out_proj",
    )(br_lru, br_attn, br_mem, zf, zf, zf, x, w0, w1, w2, wo, gpost, gpre)


FFN_CHUNK = 512


def _ffn_kernel(h_ref, x_ref, s0_ref, wup_ref, wc_ref, bc_ref, wdn_ref, g_ref,
                y_ref, cN_ref, up_ref, *, bb, tt):
    t = pl.program_id(1)
    W = FFN_CONV_W
    P = SUBLANE
    rows = bb * tt

    @pl.when(t == 0)
    def _():
        up_ref[:, P - (W - 1):P, :] = s0_ref[...]

    h = h_ref[...].reshape(rows, D_MODEL).astype(BF16)
    n_chunks = D_FF // FFN_CHUNK

    def chunk_cols(c):
        return [slice(base + c * FFN_CHUNK, base + (c + 1) * FFN_CHUNK) for base in (0, D_FF)]

    def up_project(c):
        for cs in chunk_cols(c):
            up_ref[:, P:, cs] = _dot(h, wup_ref[:, cs]).reshape(bb, tt, FFN_CHUNK)

    def activation(c):
        halves = []
        for cs in chunk_cols(c):
            uc = up_ref[:, P - 2:P - 2 + tt, cs] * wc_ref[0:1, cs]
            for j in range(1, W):
                uc = uc + up_ref[:, P - 2 + j:P - 2 + j + tt, cs] * wc_ref[j:j + 1, cs]
            halves.append((uc + bc_ref[:, cs]).reshape(rows, FFN_CHUNK))
        return (jax.nn.gelu(halves[0]) * halves[1]).astype(BF16)

    up_project(0)
    acc = jnp.zeros((rows, D_MODEL), F32)
    for c in range(n_chunks):
        if c + 1 < n_chunks:
            up_project(c + 1)
        acc = acc + _dot(activation(c), wdn_ref[c * FFN_CHUNK:(c + 1) * FFN_CHUNK, :])

    tail = up_ref[:, tt + P - (W - 1):tt + P, :]
    up_ref[:, P - (W - 1):P, :] = tail
    cN_ref[...] = tail
    y = x_ref[...].reshape(rows, D_MODEL) + _rms(acc) * g_ref[...]
    y_ref[...] = y.reshape(bb, tt, D_MODEL)


def _ffn(h3, x3, s0, wup, wc, bc, wdn, g, bb, tt):
    b, t, _ = x3.shape
    kern = functools.partial(_ffn_kernel, bb=bb, tt=tt)
    cst = lambda i, j: (0, 0)
    once = pl.Buffered(1)
    return pl.pallas_call(
        kern,
        grid=(b // bb, t // tt),
        in_specs=[
            pl.BlockSpec((bb, tt, D_MODEL), lambda i, j: (i, j, 0)),
            pl.BlockSpec((bb, tt, D_MODEL), lambda i, j: (i, j, 0)),
            pl.BlockSpec((bb, FFN_CONV_W - 1, 2 * D_FF), lambda i, j: (i, 0, 0)),
            pl.BlockSpec((D_MODEL, 2 * D_FF), cst, pipeline_mode=once),
            pl.BlockSpec((FFN_CONV_W, 2 * D_FF), cst),
            pl.BlockSpec((1, 2 * D_FF), cst),
            pl.BlockSpec((D_FF, D_MODEL), cst, pipeline_mode=once),
            pl.BlockSpec((1, D_MODEL), cst),
        ],
        out_specs=[
            pl.BlockSpec((bb, tt, D_MODEL), lambda i, j: (i, j, 0)),
            pl.BlockSpec((bb, FFN_CONV_W - 1, 2 * D_FF), lambda i, j: (i, 0, 0)),
        ],
        out_shape=[
            jax.ShapeDtypeStruct((b, t, D_MODEL), F32),
            jax.ShapeDtypeStruct((b, FFN_CONV_W - 1, 2 * D_FF), F32),
        ],
        scratch_shapes=[pltpu.VMEM((bb, tt + SUBLANE, 2 * D_FF), F32)],
        compiler_params=_params(("parallel", "arbitrary")),
        name="conv_ffn",
    )(h3, x3, s0, wup, wc, bc, wdn, g)


def _layer(x3, attn_fn, mk3, mv3, lru_h0, lru_conv0, ffn_conv0, wts, *, prompt):
    b, t, _ = x3.shape
    m = b * t
    x2 = x3.reshape(m, D_MODEL)
    act_dtype = BF16 if prompt else F32
    tm = 512
    zf, zb, k, v = _in_proj(x2, wts["g_pre_mix"], wts["w_in"], wts["w_kt"], act_dtype, tm, t,
                            k_transposed=prompt)

    bb, tt = (1, 512) if prompt else (32, t)
    br_lru, lru_h, lru_conv = _lru(
        zf.reshape(b, t, -1), lru_conv0, lru_h0.reshape(b, 1, D_MODEL),
        wts["w_lru_conv"], wts["b_lru_conv"], wts["w_rg"], wts["b_rg_a"], wts["b_rg_i"],
        wts["lru_lambda"], act_dtype, bb, tt)

    br_attn = attn_fn(zb, k, v)
    br_mem = _mem_attn(zb.reshape(b, t, -1), mk3, mv3, act_dtype, 512 if prompt else t,
                        native=not prompt)

    x1, h2 = _merge(br_lru.reshape(m, D_MODEL), br_attn.reshape(m, D_MODEL),
                    br_mem.reshape(m, D_MODEL), zf, x2,
                    wts["w_br_lru"], wts["w_br_attn"], wts["w_br_mem"], wts["w_out"],
                    wts["g_post_mix"], wts["g_pre_ffn"], act_dtype, tm)

    bb, tt = (1, 256) if prompt else (16, t)
    y, ffn_conv = _ffn(h2.reshape(b, t, D_MODEL), x1.reshape(b, t, D_MODEL), ffn_conv0,
                       wts["w_up"], wts["w_ffn_conv"], wts["b_ffn_conv"], wts["w_down"],
                       wts["g_post_ffn"], bb, tt)
    return y, k, v, lru_h.reshape(b, D_MODEL), lru_conv, ffn_conv


def _lambda_init(layer_idx):
    return 0.8 - 0.6 * math.exp(-0.3 * layer_idx)


def kernel(x_prompt, x_sample, mem_prompt, cache_k, cache_v, page_table, cache_mem_k, cache_mem_v, state_lru_h, state_lru_conv, state_ffn_conv, g_pre_mix, w_in, w_lru_conv, b_lru_conv, w_rg_a, b_rg_a, w_rg_i, b_rg_i, lru_lambda, lambda_q1, lambda_k1, lambda_q2, lambda_k2, g_subln, g_mem, w_mem_kv, w_br_lru, w_br_attn, w_br_mem, w_out, g_post_mix, g_pre_ffn, w_up, w_ffn_conv, b_ffn_conv, w_down, g_post_ffn):
    depth = w_in.shape[0]
    bp, tp, _ = x_prompt.shape
    bd, td, _ = x_sample.shape
    xp, xs = x_prompt, x_sample
    outs = [[] for _ in range(12)]
    row = lambda a: a.reshape(1, -1)

    for l in range(depth):
        lam_init = _lambda_init(l)
        wi = w_in[l]
        blk = lambda i: wi[:, i * D_MODEL:(i + 1) * D_MODEL]
        w_in_p = jnp.concatenate(
            [blk(0), blk(1), blk(6), blk(7), blk(8),
             blk(2) * (LOG2_E / math.sqrt(HEAD_DIM)), blk(3), blk(4),
             blk(5) * (1.0 / math.sqrt(MEM_HEAD_DIM))], axis=1).astype(BF16)
        wts = dict(
            g_pre_mix=row(g_pre_mix[l]), w_in=w_in_p, w_kt=blk(3).T.astype(BF16),
            w_lru_conv=w_lru_conv[l], b_lru_conv=row(b_lru_conv[l]),
            w_rg=jnp.concatenate([w_rg_a[l], w_rg_i[l]], axis=-1).astype(BF16),
            b_rg_a=row(b_rg_a[l]), b_rg_i=row(b_rg_i[l]), lru_lambda=row(lru_lambda[l]),
            w_br_lru=w_br_lru[l].astype(BF16), w_br_attn=w_br_attn[l].astype(BF16),
            w_br_mem=w_br_mem[l].astype(BF16), w_out=w_out[l].astype(BF16),
            g_post_mix=row(g_post_mix[l]), g_pre_ffn=row(g_pre_ffn[l]),
            w_up=w_up[l].astype(BF16), w_ffn_conv=w_ffn_conv[l], b_ffn_conv=row(b_ffn_conv[l]),
            w_down=w_down[l].astype(BF16), g_post_ffn=row(g_post_ffn[l]),
        )
        lams = (row(lambda_q1[l]), row(lambda_k1[l]), row(lambda_q2[l]), row(lambda_k2[l]))
        gs = row(g_subln[l])

        mkv = _mem_kv(mem_prompt.reshape(bp * N_MEM, D_MODEL), row(g_mem[l]),
                      w_mem_kv[l].astype(BF16))
        mk_p = mkv[0].reshape(bp, N_MEM, D_MODEL)
        mv_p = mkv[1].reshape(bp, N_MEM, D_MODEL)
        attn_p = lambda zb, k, v: _attn_prompt(zb, *lams, gs, bp, tp, ATTN_TQ, ATTN_HP, lam_init)
        xp, k_p, v_p, h_p, c_p, f_p = _layer(
            xp, attn_p, mk_p, mv_p,
            jnp.zeros((bp, D_MODEL), F32),
            jnp.zeros((bp, LRU_CONV_W - 1, D_MODEL), F32),
            jnp.zeros((bp, FFN_CONV_W - 1, 2 * D_FF), F32),
            wts, prompt=True)

        ck = cache_k[l].transpose(0, 2, 3, 4, 1).reshape(-1, D_MODEL, PAGE_SIZE)
        cv = cache_v[l].reshape(-1, PAGE_SIZE * N_HEADS, V_DIM)
        mem_view = lambda c: c.reshape(bd, N_MEM, MEM_HEADS, MEM_HEAD_DIM // LANE, LANE).transpose(
            0, 1, 3, 2, 4).reshape(bd, N_MEM * MEM_ROW_GROUP, LANE)
        attn_s = lambda zb, k, v: _attn_sample(
            page_table, zb.reshape(bd, td, -1), k.reshape(bd, td, D_MODEL),
            v.reshape(bd, td, D_MODEL), ck, cv, *lams, gs, lam_init)
        xs, k_s, v_s, h_s, c_s, f_s = _layer(
            xs, attn_s, mem_view(cache_mem_k[l]), mem_view(cache_mem_v[l]),
            state_lru_h[l], state_lru_conv[l], state_ffn_conv[l], wts, prompt=False)

        k_p = k_p.reshape(bp, N_HEADS, 2, HEAD_DIM, tp).transpose(0, 4, 1, 2, 3)
        vals = (k_p, v_p.reshape(bp, tp, N_HEADS, V_DIM),
                mk_p.reshape(bp, N_MEM, MEM_HEADS, MEM_HEAD_DIM),
                mv_p.reshape(bp, N_MEM, MEM_HEADS, MEM_HEAD_DIM), h_p, c_p, f_p,
                k_s.reshape(bd, td, N_HEADS, 2, HEAD_DIM), v_s.reshape(bd, td, N_HEADS, V_DIM),
                h_s, c_s, f_s)
        for o, val in zip(outs, vals):
            o.append(val)

    return (xp, xs, *[jnp.stack(o) for o in outs])
```

```python
import functools
import math

import jax
import jax.numpy as jnp
from jax import lax
from jax.experimental import pallas as pl
from jax.experimental.pallas import tpu as pltpu

F32 = jnp.float32
BF16 = jnp.bfloat16

D_MODEL = 1024
N_HEADS = 8
HEAD_DIM = 64
V_DIM = 128
N_MEM = 256
MEM_HEADS = 4
MEM_HEAD_DIM = 256
N_LRU_BLOCKS = 8
LRU_BLOCK = 128
LRU_CONV_W = 4
LRU_C = 8.0
D_FF = 3072
FFN_CONV_W = 3
RMS_EPS = 1e-6
NEG_INF = -1e30
LOG2_E = math.log2(math.e)
PAGE_SIZE = 128

N_F32_BLOCKS = 5
N_IN_BLOCKS = 9
ZB_Q, ZB_K, ZB_V, ZB_CQ = 0, 1, 2, 3

LANE = 128
SUBLANE = 8
VMEM_LIMIT = 56 * 1024 * 1024
ATTN_TQ = 512
ATTN_HP = 4
ATTN_CW = 256
ATTN_KH = 256
ATTN_AHEAD = 3


def _params(sem, vmem=VMEM_LIMIT):
    return pltpu.CompilerParams(dimension_semantics=sem, vmem_limit_bytes=vmem)


def _rms(x):
    return x * lax.rsqrt(jnp.mean(x * x, axis=-1, keepdims=True) + RMS_EPS)


def _dot(a, b):
    return jnp.dot(a, b, preferred_element_type=F32)


def _dot_nt(a, b):
    return lax.dot_general(a, b, (((1,), (1,)), ((), ())), preferred_element_type=F32)


def _dot_tn(a, b):
    return lax.dot_general(a, b, (((0,), (0,)), ((), ())), preferred_element_type=F32)


def _lam(q1, k1, q2, k2, lam_init):
    s1 = jnp.sum(q1 * k1, axis=-1, keepdims=True)
    s2 = jnp.sum(q2 * k2, axis=-1, keepdims=True)
    return jnp.exp(s1) - jnp.exp(s2) + lam_init


def _in_proj_kernel(x_ref, g_ref, w_ref, wkt_ref, zf_ref, zb_ref, k_ref, v_ref, h_ref, *,
                    k_transposed):
    n = pl.program_id(1)

    @pl.when(n == 0)
    def _():
        h_ref[...] = (_rms(x_ref[...]) * g_ref[...]).astype(BF16)

    @pl.when(n < N_F32_BLOCKS)
    def _():
        zf_ref[...] = _dot(h_ref[...], w_ref[n])

    @pl.when((n == N_F32_BLOCKS + ZB_Q) | (n == N_F32_BLOCKS + ZB_CQ))
    def _():
        zb_ref[...] = _dot(h_ref[...], w_ref[n]).astype(zb_ref.dtype)

    @pl.when(n == N_F32_BLOCKS + ZB_K)
    def _():
        z = _dot(h_ref[...], w_ref[N_F32_BLOCKS + ZB_K])
        zb_ref[...] = z.astype(zb_ref.dtype)
        if k_transposed:
            k_ref[0] = _dot_nt(wkt_ref[...], h_ref[...])
        else:
            k_ref[...] = z

    @pl.when(n == N_F32_BLOCKS + ZB_V)
    def _():
        z = _dot(h_ref[...], w_ref[N_F32_BLOCKS + ZB_V])
        zb_ref[...] = z.astype(zb_ref.dtype)
        v_ref[...] = z


def _in_proj(x, g, w_in, w_kt, zb_dtype, tm, seq_len, k_transposed):
    m = x.shape[0]
    nb = N_F32_BLOCKS
    nt = seq_len // tm
    if k_transposed:
        k_spec = pl.BlockSpec((1, D_MODEL, tm), lambda i, n: (i // nt, 0, i % nt))
        k_shape = jax.ShapeDtypeStruct((m // seq_len, D_MODEL, seq_len), F32)
    else:
        k_spec = pl.BlockSpec((tm, D_MODEL), lambda i, n: (i, 0))
        k_shape = jax.ShapeDtypeStruct((m, D_MODEL), F32)
    return pl.pallas_call(
        functools.partial(_in_proj_kernel, k_transposed=k_transposed),
        grid=(m // tm, N_IN_BLOCKS),
        in_specs=[
            pl.BlockSpec((tm, D_MODEL), lambda i, n: (i, 0)),
            pl.BlockSpec((1, D_MODEL), lambda i, n: (0, 0)),
            pl.BlockSpec((N_IN_BLOCKS, D_MODEL, D_MODEL), lambda i, n: (0, 0, 0),
                         pipeline_mode=pl.Buffered(1)),
            pl.BlockSpec((D_MODEL, D_MODEL), lambda i, n: (0, 0), pipeline_mode=pl.Buffered(1)),
        ],
        out_specs=[
            pl.BlockSpec((tm, D_MODEL), lambda i, n: (i, jnp.minimum(n, nb - 1))),
            pl.BlockSpec((tm, D_MODEL), lambda i, n: (i, jnp.maximum(n - nb, 0))),
            k_spec,
            pl.BlockSpec((tm, D_MODEL), lambda i, n: (i, 0)),
        ],
        out_shape=[
            jax.ShapeDtypeStruct((m, nb * D_MODEL), F32),
            jax.ShapeDtypeStruct((m, (N_IN_BLOCKS - nb) * D_MODEL), zb_dtype),
            k_shape,
            jax.ShapeDtypeStruct((m, D_MODEL), F32),
        ],
        scratch_shapes=[pltpu.VMEM((tm, D_MODEL), BF16)],
        compiler_params=_params(("parallel", "arbitrary")),
        name="in_proj",
    )(x, g, w_in, w_kt)


def _lru_kernel(lx_ref, ly_ref, s0_ref, h0_ref, wc_ref, bc_ref, wrg_ref, ba_ref, bi_ref,
                lam_ref, out_ref, hN_ref, cN_ref, xp_ref, a_ref, u_ref, hc_ref, *, bb, tt):
    t = pl.program_id(1)
    W = LRU_CONV_W
    P = SUBLANE

    @pl.when(t == 0)
    def _():
        xp_ref[:, P - (W - 1):P, :] = s0_ref[...]
        hc_ref[...] = h0_ref[...]

    xp_ref[:, P:, :] = lx_ref[...]
    xc = xp_ref[:, P - 3:P - 3 + tt, :] * wc_ref[0:1, :]
    for j in range(1, W):
        xc = xc + xp_ref[:, P - 3 + j:P - 3 + j + tt, :] * wc_ref[j:j + 1, :]
    xc = xc + bc_ref[...]
    tail = xp_ref[:, tt + P - (W - 1):tt + P, :]
    xp_ref[:, P - (W - 1):P, :] = tail
    cN_ref[...] = tail

    x2 = xc.reshape(bb * tt, D_MODEL)
    xb = x2.astype(BF16)
    lin = [_dot(xb[:, n * LRU_BLOCK:(n + 1) * LRU_BLOCK], wrg_ref[n])
           for n in range(N_LRU_BLOCKS)]
    r_lin = jnp.concatenate([z[:, :LRU_BLOCK] for z in lin], axis=-1)
    i_lin = jnp.concatenate([z[:, LRU_BLOCK:] for z in lin], axis=-1)
    r = jax.nn.sigmoid(r_lin + ba_ref[...])
    gi = jax.nn.sigmoid(i_lin + bi_ref[...])
    neg_lam = -lam_ref[...]
    softplus = jnp.maximum(neg_lam, 0.0) + jnp.log1p(jnp.exp(-jnp.abs(neg_lam)))
    log_a = (-LRU_C * softplus) * r
    a = jnp.exp(log_a)
    u = jnp.sqrt(1.0 - a * a) * (gi * x2)

    g8 = (bb * tt) // SUBLANE
    a3 = a.reshape(g8, SUBLANE, D_MODEL)
    u3 = u.reshape(g8, SUBLANE, D_MODEL)
    row = lax.broadcasted_iota(jnp.int32, (g8, SUBLANE, D_MODEL), 1)
    for s in (1, 2, 4):
        a_sh = pltpu.roll(a3, s, 1)
        u_sh = pltpu.roll(u3, s, 1)
        ok = row >= s
        u3 = jnp.where(ok, a3 * u_sh + u3, u3)
        a3 = jnp.where(ok, a3 * a_sh, a3)
    a_ref[...] = a3.reshape(bb, tt, D_MODEL)
    u_ref[...] = u3.reshape(bb, tt, D_MODEL)

    def body(g, h):
        off = pl.multiple_of(g * SUBLANE, SUBLANE)
        hg = a_ref[:, pl.ds(off, SUBLANE), :] * h + u_ref[:, pl.ds(off, SUBLANE), :]
        u_ref[:, pl.ds(off, SUBLANE), :] = hg
        return hg[:, SUBLANE - 1:SUBLANE, :]

    h_last = lax.fori_loop(0, tt // SUBLANE, body, hc_ref[...])
    hc_ref[...] = h_last
    hN_ref[...] = h_last
    out_ref[...] = (jax.nn.gelu(ly_ref[...]) * u_ref[...]).astype(out_ref.dtype)


def _lru(zf3, s0, h0, wc, bc, wrg, ba, bi, lam, out_dtype, bb, tt):
    b, t, _ = zf3.shape
    kern = functools.partial(_lru_kernel, bb=bb, tt=tt)
    const2 = lambda i, j: (0, 0)
    return pl.pallas_call(
        kern,
        grid=(b // bb, t // tt),
        in_specs=[
            pl.BlockSpec((bb, tt, D_MODEL), lambda i, j: (i, j, 0)),
            pl.BlockSpec((bb, tt, D_MODEL), lambda i, j: (i, j, 1)),
            pl.BlockSpec((bb, LRU_CONV_W - 1, D_MODEL), lambda i, j: (i, 0, 0)),
            pl.BlockSpec((bb, 1, D_MODEL), lambda i, j: (i, 0, 0)),
            pl.BlockSpec((LRU_CONV_W, D_MODEL), const2),
            pl.BlockSpec((1, D_MODEL), const2),
            pl.BlockSpec((N_LRU_BLOCKS, LRU_BLOCK, 2 * LRU_BLOCK), lambda i, j: (0, 0, 0)),
            pl.BlockSpec((1, D_MODEL), const2),
            pl.BlockSpec((1, D_MODEL), const2),
            pl.BlockSpec((1, D_MODEL), const2),
        ],
        out_specs=[
            pl.BlockSpec((bb, tt, D_MODEL), lambda i, j: (i, j, 0)),
            pl.BlockSpec((bb, 1, D_MODEL), lambda i, j: (i, 0, 0)),
            pl.BlockSpec((bb, LRU_CONV_W - 1, D_MODEL), lambda i, j: (i, 0, 0)),
        ],
        out_shape=[
            jax.ShapeDtypeStruct((b, t, D_MODEL), out_dtype),
            jax.ShapeDtypeStruct((b, 1, D_MODEL), F32),
            jax.ShapeDtypeStruct((b, LRU_CONV_W - 1, D_MODEL), F32),
        ],
        scratch_shapes=[
            pltpu.VMEM((bb, tt + SUBLANE, D_MODEL), F32),
            pltpu.VMEM((bb, tt, D_MODEL), F32),
            pltpu.VMEM((bb, tt, D_MODEL), F32),
            pltpu.VMEM((bb, 1, D_MODEL), F32),
        ],
        compiler_params=_params(("parallel", "arbitrary")),
        name="rg_lru",
    )(zf3, zf3, s0, h0, wc, bc, wrg, ba, bi, lam)


def _attn_prompt_kernel(q_ref, k_ref, v_ref, lq1_ref, lk1_ref, lq2_ref, lk2_ref, gs_ref,
                        o_ref, qq_ref, m_ref, l_ref, acc_ref, *, tq, hp, lam_init):
    qi = pl.program_id(2)
    for hh in range(hp):
        q = q_ref[:, hh * V_DIM:(hh + 1) * V_DIM]
        lane = lax.broadcasted_iota(jnp.int32, q.shape, 1)
        zero = jnp.zeros_like(q)
        qq_ref[hh, :tq, :] = jnp.where(lane < HEAD_DIM, q, zero)
        qq_ref[hh, tq:, :] = jnp.where(lane >= HEAD_DIM, q, zero)
    m_ref[...] = jnp.full(m_ref.shape, NEG_INF, F32)
    l_ref[...] = jnp.zeros(l_ref.shape, F32)
    acc_ref[...] = jnp.zeros(acc_ref.shape, F32)

    def step(ki, diagonal):
        off = pl.multiple_of(ki * tq, tq)
        chunks = [(hh, c0) for hh in range(hp) for c0 in range(0, 2 * tq, ATTN_CW)]

        def n_keys(c0):
            return min(tq, c0 % tq + ATTN_CW) if diagonal else tq

        def scores(hh, c0):
            qc = qq_ref[hh, c0:c0 + ATTN_CW, :]
            s = jnp.concatenate(
                [_dot_nt(k_ref[pl.ds(off + r0, ATTN_KH), hh * V_DIM:(hh + 1) * V_DIM], qc)
                 for r0 in range(0, n_keys(c0), ATTN_KH)], axis=0)
            if diagonal:
                key = lax.broadcasted_iota(jnp.int32, s.shape, 0)
                qry = lax.broadcasted_iota(jnp.int32, s.shape, 1) + c0 % tq
                s = jnp.where(key <= qry, s, NEG_INF)
            return s

        def softmax(hh, c0, s):
            cols = slice(c0, c0 + ATTN_CW)
            m_old = m_ref[hh, :, cols]
            m_new = jnp.maximum(m_old, jnp.max(s, axis=0, keepdims=True))
            alpha = jnp.exp2(m_old - m_new)
            p = jnp.exp2(s - m_new)
            l_ref[hh, :, cols] = alpha * l_ref[hh, :, cols] + jnp.sum(p, axis=0, keepdims=True)
            m_ref[hh, :, cols] = m_new
            return alpha, p.astype(BF16)

        def values(hh, c0, alpha, p):
            cols = slice(c0, c0 + ATTN_CW)
            pv = None
            for r0 in range(0, n_keys(c0), ATTN_KH):
                vt = v_ref[pl.ds(off + r0, ATTN_KH), hh * V_DIM:(hh + 1) * V_DIM]
                part = _dot_tn(vt, p[r0:r0 + ATTN_KH, :])
                pv = part if pv is None else pv + part
            acc_ref[hh, :, cols] = alpha * acc_ref[hh, :, cols] + pv

        n = len(chunks)
        s_q = {i: scores(*chunks[i]) for i in range(min(ATTN_AHEAD, n))}
        p_q = {0: softmax(*chunks[0], s_q.pop(0))}
        for i in range(n):
            if i + ATTN_AHEAD < n:
                s_q[i + ATTN_AHEAD] = scores(*chunks[i + ATTN_AHEAD])
            if i + 1 < n:
                p_q[i + 1] = softmax(*chunks[i + 1], s_q.pop(i + 1))
            values(*chunks[i], *p_q.pop(i))

    def body(ki, c):
        step(ki, False)
        return c

    lax.fori_loop(0, qi, body, 0)
    step(qi, True)

    lam = _lam(lq1_ref[...], lk1_ref[...], lq2_ref[...], lk2_ref[...], lam_init)
    for hh in range(hp):
        o_t = acc_ref[hh] * (1.0 / l_ref[hh])
        o_t = o_t[:, :tq] - lam * o_t[:, tq:]
        o = o_t.T
        o_ref[:, hh * V_DIM:(hh + 1) * V_DIM] = (
            (_rms(o) * gs_ref[...]) * (1.0 - lam_init)).astype(o_ref.dtype)


def _attn_prompt(zb, lq1, lk1, lq2, lk2, gs, b, t, tq, hp, lam_init):
    m = zb.shape[0]
    nq = t // tq
    ng = N_HEADS // hp
    w = hp * V_DIM
    kern = functools.partial(_attn_prompt_kernel, tq=tq, hp=hp, lam_init=lam_init)
    vec = lambda i, h, j: (0, 0)
    return pl.pallas_call(
        kern,
        grid=(b, ng, nq),
        in_specs=[
            pl.BlockSpec((tq, w), lambda i, h, j: (i * nq + j, ZB_Q * ng + h)),
            pl.BlockSpec((t, w), lambda i, h, j: (i, ZB_K * ng + h), pipeline_mode=pl.Buffered(1)),
            pl.BlockSpec((t, w), lambda i, h, j: (i, ZB_V * ng + h), pipeline_mode=pl.Buffered(1)),
            pl.BlockSpec((1, HEAD_DIM), vec),
            pl.BlockSpec((1, HEAD_DIM), vec),
            pl.BlockSpec((1, HEAD_DIM), vec),
            pl.BlockSpec((1, HEAD_DIM), vec),
            pl.BlockSpec((1, V_DIM), vec),
        ],
        out_specs=pl.BlockSpec((tq, w), lambda i, h, j: (i * nq + j, h)),
        out_shape=jax.ShapeDtypeStruct((m, N_HEADS * V_DIM), BF16),
        scratch_shapes=[
            pltpu.VMEM((hp, 2 * tq, V_DIM), BF16),
            pltpu.VMEM((hp, 1, 2 * tq), F32),
            pltpu.VMEM((hp, 1, 2 * tq), F32),
            pltpu.VMEM((hp, V_DIM, 2 * tq), F32),
        ],
        compiler_params=_params(("parallel", "parallel", "arbitrary")),
        name="diff_attn_prompt",
    )(zb, zb, zb, lq1, lk1, lq2, lk2, gs)


def _attn_sample_kernel(pt_ref, q_ref, kn_ref, vn_ref, *rest, n_pages, tn, lam_init):
    del pt_ref
    kp_refs = rest[:n_pages]
    vp_refs = rest[n_pages:2 * n_pages]
    lq1_ref, lk1_ref, lq2_ref, lk2_ref, gs_ref, o_ref, s_ref = rest[2 * n_pages:]
    nrow = 2 * N_HEADS * tn
    n_past = n_pages * PAGE_SIZE
    HPB = 2
    nblk = N_HEADS // HPB
    rb = nrow // nblk
    fb = HPB * V_DIM

    q = q_ref[0]
    qt = jnp.concatenate([q] * (nrow // tn), axis=0)
    rj = lax.broadcasted_iota(jnp.int32, qt.shape, 0)
    cf = lax.broadcasted_iota(jnp.int32, qt.shape, 1)
    pair_j = rj >> (tn.bit_length() - 1)
    feat_blk = cf >> (HEAD_DIM.bit_length() - 1)
    qbd = jnp.where(feat_blk == pair_j, qt, 0.0)
    qblk = [qbd[g * rb:(g + 1) * rb, g * fb:(g + 1) * fb] for g in range(nblk)]

    for p in range(n_pages):
        for g in range(nblk):
            s_ref[g * rb:(g + 1) * rb, p * PAGE_SIZE:(p + 1) * PAGE_SIZE] = _dot(
                qblk[g], kp_refs[p][0, g * fb:(g + 1) * fb, :])
    s_new = _dot_nt(qbd, kn_ref[0])
    qry = lax.broadcasted_iota(jnp.int32, s_new.shape, 0) & (tn - 1)
    key = lax.broadcasted_iota(jnp.int32, s_new.shape, 1)
    s_new = jnp.where(key <= qry, s_new, NEG_INF)

    s = s_ref[...]
    mx = jnp.maximum(jnp.max(s, axis=-1, keepdims=True), jnp.max(s_new, axis=-1, keepdims=True))
    e = jnp.exp2(s - mx)
    e_new = jnp.exp2(s_new - mx)
    inv_l = 1.0 / (jnp.sum(e, axis=-1, keepdims=True) + jnp.sum(e_new, axis=-1, keepdims=True))
    s_ref[...] = e * inv_l
    p_new = e_new * inv_l

    lam = _lam(lq1_ref[...], lk1_ref[...], lq2_ref[...], lk2_ref[...], lam_init)
    outs = []
    for g in range(nblk):
        rows = slice(g * rb, (g + 1) * rb)
        acc = _dot(p_new[rows, :], vn_ref[0, :, g * fb:(g + 1) * fb])
        for p in range(n_pages):
            vg = jnp.concatenate(
                [vp_refs[p][0, pl.ds(g * HPB + hl, PAGE_SIZE, stride=N_HEADS), :]
                 for hl in range(HPB)], axis=-1)
            acc = acc + _dot(s_ref[rows, p * PAGE_SIZE:(p + 1) * PAGE_SIZE], vg)
        for hl in range(HPB):
            r0 = hl * 2 * tn
            cs = slice(hl * V_DIM, (hl + 1) * V_DIM)
            o = acc[r0:r0 + tn, cs] - lam * acc[r0 + tn:r0 + 2 * tn, cs]
            outs.append((_rms(o) * gs_ref[...]) * (1.0 - lam_init))
    o_ref[0] = jnp.concatenate(outs, axis=-1).astype(o_ref.dtype)


def _attn_sample(page_table, zb3, k3, v3, cache_kt, cache_v2, lq1, lk1, lq2, lk2, gs, lam_init):
    bd, tn, _ = k3.shape
    n_pages = page_table.shape[1]
    kern = functools.partial(_attn_sample_kernel, n_pages=n_pages, tn=tn, lam_init=lam_init)
    vec = lambda i, pt: (0, 0)

    def page_spec(p):
        return pl.BlockSpec((1, D_MODEL, PAGE_SIZE), lambda i, pt: (pt[i, p], 0, 0))

    grid_spec = pltpu.PrefetchScalarGridSpec(
        num_scalar_prefetch=1,
        grid=(bd,),
        in_specs=[
            pl.BlockSpec((1, tn, D_MODEL), lambda i, pt: (i, 0, ZB_Q)),
            pl.BlockSpec((1, tn, D_MODEL), lambda i, pt: (i, 0, 0)),
            pl.BlockSpec((1, tn, D_MODEL), lambda i, pt: (i, 0, 0)),
            *[page_spec(p) for p in range(n_pages)],
            *[page_spec(p) for p in range(n_pages)],
            pl.BlockSpec((1, HEAD_DIM), vec),
            pl.BlockSpec((1, HEAD_DIM), vec),
            pl.BlockSpec((1, HEAD_DIM), vec),
            pl.BlockSpec((1, HEAD_DIM), vec),
            pl.BlockSpec((1, V_DIM), vec),
        ],
        out_specs=pl.BlockSpec((1, tn, D_MODEL), lambda i, pt: (i, 0, 0)),
        scratch_shapes=[pltpu.VMEM((2 * N_HEADS * tn, n_pages * PAGE_SIZE), F32)],
    )
    return pl.pallas_call(
        kern,
        grid_spec=grid_spec,
        out_shape=jax.ShapeDtypeStruct((bd, tn, D_MODEL), F32),
        compiler_params=_params(("arbitrary",)),
        name="diff_attn_sample",
    )(page_table, zb3, k3, v3, *([cache_kt] * n_pages), *([cache_v2] * n_pages),
      lq1, lk1, lq2, lk2, gs)


def _mem_kv_kernel(x_ref, g_ref, w_ref, o_ref):
    h = (_rms(x_ref[...]) * g_ref[...]).astype(BF16)
    o_ref[0] = _dot(h, w_ref[...])


def _mem_kv(mem2, g, w):
    m = mem2.shape[0]
    n = w.shape[1]
    return pl.pallas_call(
        _mem_kv_kernel,
        grid=(n // D_MODEL,),
        in_specs=[
            pl.BlockSpec((m, D_MODEL), lambda j: (0, 0)),
            pl.BlockSpec((1, D_MODEL), lambda j: (0, 0)),
            pl.BlockSpec((D_MODEL, D_MODEL), lambda j: (0, j)),
        ],
        out_specs=pl.BlockSpec((1, m, D_MODEL), lambda j: (j, 0, 0)),
        out_shape=jax.ShapeDtypeStruct((n // D_MODEL, m, D_MODEL), F32),
        compiler_params=_params(("parallel",)),
        name="mem_kv",
    )(mem2, g, w)


MEM_ROW_GROUP = MEM_HEADS * (MEM_HEAD_DIM // LANE)


def _mem_attn_kernel(q_ref, mk_ref, mv_ref, o_ref, *, native):
    mxu_dtype = q_ref.dtype
    q = q_ref[0]

    def head(ref, h):
        if native:
            return jnp.concatenate(
                [ref[0, pl.ds(dc * MEM_HEADS + h, N_MEM, stride=MEM_ROW_GROUP), :]
                 for dc in range(MEM_HEAD_DIM // LANE)], axis=-1)
        return ref[0, :, h * MEM_HEAD_DIM:(h + 1) * MEM_HEAD_DIM].astype(mxu_dtype)

    scores = [_dot_nt(q[:, h * MEM_HEAD_DIM:(h + 1) * MEM_HEAD_DIM], head(mk_ref, h))
              for h in range(MEM_HEADS)]
    probs = []
    for s in scores:
        e = jnp.exp(s - jnp.max(s, axis=-1, keepdims=True))
        probs.append((e / jnp.sum(e, axis=-1, keepdims=True)).astype(mxu_dtype))
    outs = [_dot(probs[h], head(mv_ref, h)) for h in range(MEM_HEADS)]
    o_ref[0] = jnp.concatenate(outs, axis=-1).astype(o_ref.dtype)


def _mem_attn(zb3, mk3, mv3, out_dtype, tq, native):
    b, t, _ = zb3.shape
    mem_block = (1,) + mk3.shape[1:]
    return pl.pallas_call(
        functools.partial(_mem_attn_kernel, native=native),
        grid=(b, t // tq),
        in_specs=[
            pl.BlockSpec((1, tq, D_MODEL), lambda i, j: (i, j, ZB_CQ)),
            pl.BlockSpec(mem_block, lambda i, j: (i, 0, 0)),
            pl.BlockSpec(mem_block, lambda i, j: (i, 0, 0)),
        ],
        out_specs=pl.BlockSpec((1, tq, D_MODEL), lambda i, j: (i, j, 0)),
        out_shape=jax.ShapeDtypeStruct((b, t, D_MODEL), out_dtype),
        compiler_params=_params(("parallel", "arbitrary")),
        name="mem_attn",
    )(zb3, mk3, mv3)


def _merge_kernel(bl_ref, ba_ref, bm_ref, g0_ref, g1_ref, g2_ref, x_ref,
                  w0_ref, w1_ref, w2_ref, wo_ref, gpost_ref, gpre_ref, x1_ref, h2_ref):
    m = jax.nn.sigmoid(g0_ref[...]) * _dot(bl_ref[...].astype(BF16), w0_ref[...])
    m = m + jax.nn.sigmoid(g1_ref[...]) * _dot(ba_ref[...].astype(BF16), w1_ref[...])
    m = m + jax.nn.sigmoid(g2_ref[...]) * _dot(bm_ref[...].astype(BF16), w2_ref[...])
    y = _dot(m.astype(BF16), wo_ref[...])
    x1 = x_ref[...] + _rms(y) * gpost_ref[...]
    x1_ref[...] = x1
    h2_ref[...] = (_rms(x1) * gpre_ref[...]).astype(h2_ref.dtype)


def _merge(br_lru, br_attn, br_mem, zf, x, w0, w1, w2, wo, gpost, gpre, h2_dtype, tm):
    m = x.shape[0]
    tok = lambda i: (i, 0)
    cst = lambda i: (0, 0)
    wspec = pl.BlockSpec((D_MODEL, D_MODEL), cst)
    return pl.pallas_call(
        _merge_kernel,
        grid=(m // tm,),
        in_specs=[
            pl.BlockSpec((tm, D_MODEL), tok),
            pl.BlockSpec((tm, D_MODEL), tok),
            pl.BlockSpec((tm, D_MODEL), tok),
            pl.BlockSpec((tm, D_MODEL), lambda i: (i, 2)),
            pl.BlockSpec((tm, D_MODEL), lambda i: (i, 3)),
            pl.BlockSpec((tm, D_MODEL), lambda i: (i, 4)),
            pl.BlockSpec((tm, D_MODEL), tok),
            wspec, wspec, wspec, wspec,
            pl.BlockSpec((1, D_MODEL), cst),
            pl.BlockSpec((1, D_MODEL), cst),
        ],
        out_specs=[pl.BlockSpec((tm, D_MODEL), tok), pl.BlockSpec((tm, D_MODEL), tok)],
        out_shape=[jax.ShapeDtypeStruct((m, D_MODEL), F32),
                   jax.ShapeDtypeStruct((m, D_MODEL), h2_dtype)],
        compiler_params=_params(("parallel",)),
        name="merge_out_proj",
    )(br_lru, br_attn, br_mem, zf, zf, zf, x, w0, w1, w2, wo, gpost, gpre)


FFN_CHUNK = 512


def _ffn_kernel(h_ref, x_ref, s0_ref, wup_ref, wc_ref, bc_ref, wdn_ref, g_ref,
                y_ref, cN_ref, up_ref, *, bb, tt):
    t = pl.program_id(1)
    W = FFN_CONV_W
    P = SUBLANE
    rows = bb * tt

    @pl.when(t == 0)
    def _():
        up_ref[:, P - (W - 1):P, :] = s0_ref[...]

    h = h_ref[...].reshape(rows, D_MODEL).astype(BF16)
    n_chunks = D_FF // FFN_CHUNK

    def chunk_cols(c):
        return [slice(base + c * FFN_CHUNK, base + (c + 1) * FFN_CHUNK) for base in (0, D_FF)]

    def up_project(c):
        for cs in chunk_cols(c):
            up_ref[:, P:, cs] = _dot(h, wup_ref[:, cs]).reshape(bb, tt, FFN_CHUNK)

    def activation(c):
        halves = []
        for cs in chunk_cols(c):
            uc = up_ref[:, P - 2:P - 2 + tt, cs] * wc_ref[0:1, cs]
            for j in range(1, W):
                uc = uc + up_ref[:, P - 2 + j:P - 2 + j + tt, cs] * wc_ref[j:j + 1, cs]
            halves.append((uc + bc_ref[:, cs]).reshape(rows, FFN_CHUNK))
        return (jax.nn.gelu(halves[0]) * halves[1]).astype(BF16)

    up_project(0)
    acc = jnp.zeros((rows, D_MODEL), F32)
    for c in range(n_chunks):
        if c + 1 < n_chunks:
            up_project(c + 1)
        acc = acc + _dot(activation(c), wdn_ref[c * FFN_CHUNK:(c + 1) * FFN_CHUNK, :])

    tail = up_ref[:, tt + P - (W - 1):tt + P, :]
    up_ref[:, P - (W - 1):P, :] = tail
    cN_ref[...] = tail
    y = x_ref[...].reshape(rows, D_MODEL) + _rms(acc) * g_ref[...]
    y_ref[...] = y.reshape(bb, tt, D_MODEL)


def _ffn(h3, x3, s0, wup, wc, bc, wdn, g, bb, tt):
    b, t, _ = x3.shape
    kern = functools.partial(_ffn_kernel, bb=bb, tt=tt)
    cst = lambda i, j: (0, 0)
    once = pl.Buffered(1)
    return pl.pallas_call(
        kern,
        grid=(b // bb, t // tt),
        in_specs=[
            pl.BlockSpec((bb, tt, D_MODEL), lambda i, j: (i, j, 0)),
            pl.BlockSpec((bb, tt, D_MODEL), lambda i, j: (i, j, 0)),
            pl.BlockSpec((bb, FFN_CONV_W - 1, 2 * D_FF), lambda i, j: (i, 0, 0)),
            pl.BlockSpec((D_MODEL, 2 * D_FF), cst, pipeline_mode=once),
            pl.BlockSpec((FFN_CONV_W, 2 * D_FF), cst),
            pl.BlockSpec((1, 2 * D_FF), cst),
            pl.BlockSpec((D_FF, D_MODEL), cst, pipeline_mode=once),
            pl.BlockSpec((1, D_MODEL), cst),
        ],
        out_specs=[
            pl.BlockSpec((bb, tt, D_MODEL), lambda i, j: (i, j, 0)),
            pl.BlockSpec((bb, FFN_CONV_W - 1, 2 * D_FF), lambda i, j: (i, 0, 0)),
        ],
        out_shape=[
            jax.ShapeDtypeStruct((b, t, D_MODEL), F32),
            jax.ShapeDtypeStruct((b, FFN_CONV_W - 1, 2 * D_FF), F32),
        ],
        scratch_shapes=[pltpu.VMEM((bb, tt + SUBLANE, 2 * D_FF), F32)],
        compiler_params=_params(("parallel", "arbitrary")),
        name="conv_ffn",
    )(h3, x3, s0, wup, wc, bc, wdn, g)


def _layer(x3, attn_fn, mk3, mv3, lru_h0, lru_conv0, ffn_conv0, wts, *, prompt):
    b, t, _ = x3.shape
    m = b * t
    x2 = x3.reshape(m, D_MODEL)
    act_dtype = BF16 if prompt else F32
    tm = 512
    zf, zb, k, v = _in_proj(x2, wts["g_pre_mix"], wts["w_in"], wts["w_kt"], act_dtype, tm, t,
                            k_transposed=prompt)

    bb, tt = (1, 512) if prompt else (32, t)
    br_lru, lru_h, lru_conv = _lru(
        zf.reshape(b, t, -1), lru_conv0, lru_h0.reshape(b, 1, D_MODEL),
        wts["w_lru_conv"], wts["b_lru_conv"], wts["w_rg"], wts["b_rg_a"], wts["b_rg_i"],
        wts["lru_lambda"], act_dtype, bb, tt)

    br_attn = attn_fn(zb, k, v)
    br_mem = _mem_attn(zb.reshape(b, t, -1), mk3, mv3, act_dtype, 512 if prompt else t,
                        native=not prompt)

    x1, h2 = _merge(br_lru.reshape(m, D_MODEL), br_attn.reshape(m, D_MODEL),
                    br_mem.reshape(m, D_MODEL), zf, x2,
                    wts["w_br_lru"], wts["w_br_attn"], wts["w_br_mem"], wts["w_out"],
                    wts["g_post_mix"], wts["g_pre_ffn"], act_dtype, tm)

    bb, tt = (1, 256) if prompt else (16, t)
    y, ffn_conv = _ffn(h2.reshape(b, t, D_MODEL), x1.reshape(b, t, D_MODEL), ffn_conv0,
                       wts["w_up"], wts["w_ffn_conv"], wts["b_ffn_conv"], wts["w_down"],
                       wts["g_post_ffn"], bb, tt)
    return y, k, v, lru_h.reshape(b, D_MODEL), lru_conv, ffn_conv


def _lambda_init(layer_idx):
    return 0.8 - 0.6 * math.exp(-0.3 * layer_idx)


def kernel(x_prompt, x_sample, mem_prompt, cache_k, cache_v, page_table, cache_mem_k, cache_mem_v, state_lru_h, state_lru_conv, state_ffn_conv, g_pre_mix, w_in, w_lru_conv, b_lru_conv, w_rg_a, b_rg_a, w_rg_i, b_rg_i, lru_lambda, lambda_q1, lambda_k1, lambda_q2, lambda_k2, g_subln, g_mem, w_mem_kv, w_br_lru, w_br_attn, w_br_mem, w_out, g_post_mix, g_pre_ffn, w_up, w_ffn_conv, b_ffn_conv, w_down, g_post_ffn):
    depth = w_in.shape[0]
    bp, tp, _ = x_prompt.shape
    bd, td, _ = x_sample.shape
    xp, xs = x_prompt, x_sample
    outs = [[] for _ in range(12)]
    row = lambda a: a.reshape(1, -1)

    for l in range(depth):
        lam_init = _lambda_init(l)
        wi = w_in[l]
        blk = lambda i: wi[:, i * D_MODEL:(i + 1) * D_MODEL]
        w_in_p = jnp.stack(
            [blk(0), blk(1), blk(6), blk(7), blk(8),
             blk(2) * (LOG2_E / math.sqrt(HEAD_DIM)), blk(3), blk(4),
             blk(5) * (1.0 / math.sqrt(MEM_HEAD_DIM))]).astype(BF16)
        wts = dict(
            g_pre_mix=row(g_pre_mix[l]), w_in=w_in_p, w_kt=blk(3).T.astype(BF16),
            w_lru_conv=w_lru_conv[l], b_lru_conv=row(b_lru_conv[l]),
            w_rg=jnp.concatenate([w_rg_a[l], w_rg_i[l]], axis=-1).astype(BF16),
            b_rg_a=row(b_rg_a[l]), b_rg_i=row(b_rg_i[l]), lru_lambda=row(lru_lambda[l]),
            w_br_lru=w_br_lru[l].astype(BF16), w_br_attn=w_br_attn[l].astype(BF16),
            w_br_mem=w_br_mem[l].astype(BF16), w_out=w_out[l].astype(BF16),
            g_post_mix=row(g_post_mix[l]), g_pre_ffn=row(g_pre_ffn[l]),
            w_up=w_up[l].astype(BF16), w_ffn_conv=w_ffn_conv[l], b_ffn_conv=row(b_ffn_conv[l]),
            w_down=w_down[l].astype(BF16), g_post_ffn=row(g_post_ffn[l]),
        )
        lams = (row(lambda_q1[l]), row(lambda_k1[l]), row(lambda_q2[l]), row(lambda_k2[l]))
        gs = row(g_subln[l])

        mkv = _mem_kv(mem_prompt.reshape(bp * N_MEM, D_MODEL), row(g_mem[l]),
                      w_mem_kv[l].astype(BF16))
        mk_p = mkv[0].reshape(bp, N_MEM, D_MODEL)
        mv_p = mkv[1].reshape(bp, N_MEM, D_MODEL)
        attn_p = lambda zb, k, v: _attn_prompt(zb, *lams, gs, bp, tp, ATTN_TQ, ATTN_HP, lam_init)
        xp, k_p, v_p, h_p, c_p, f_p = _layer(
            xp, attn_p, mk_p, mv_p,
            jnp.zeros((bp, D_MODEL), F32),
            jnp.zeros((bp, LRU_CONV_W - 1, D_MODEL), F32),
            jnp.zeros((bp, FFN_CONV_W - 1, 2 * D_FF), F32),
            wts, prompt=True)

        ck = cache_k[l].transpose(0, 2, 3, 4, 1).reshape(-1, D_MODEL, PAGE_SIZE)
        cv = cache_v[l].reshape(-1, PAGE_SIZE * N_HEADS, V_DIM)
        mem_view = lambda c: c.reshape(bd, N_MEM, MEM_HEADS, MEM_HEAD_DIM // LANE, LANE).transpose(
            0, 1, 3, 2, 4).reshape(bd, N_MEM * MEM_ROW_GROUP, LANE)
        attn_s = lambda zb, k, v: _attn_sample(
            page_table, zb.reshape(bd, td, -1), k.reshape(bd, td, D_MODEL),
            v.reshape(bd, td, D_MODEL), ck, cv, *lams, gs, lam_init)
        xs, k_s, v_s, h_s, c_s, f_s = _layer(
            xs, attn_s, mem_view(cache_mem_k[l]), mem_view(cache_mem_v[l]),
            state_lru_h[l], state_lru_conv[l], state_ffn_conv[l], wts, prompt=False)

        k_p = k_p.reshape(bp, N_HEADS, 2, HEAD_DIM, tp).transpose(0, 4, 1, 2, 3)
        vals = (k_p, v_p.reshape(bp, tp, N_HEADS, V_DIM),
                mk_p.reshape(bp, N_MEM, MEM_HEADS, MEM_HEAD_DIM),
                mv_p.reshape(bp, N_MEM, MEM_HEADS, MEM_HEAD_DIM), h_p, c_p, f_p,
                k_s.reshape(bd, td, N_HEADS, 2, HEAD_DIM), v_s.reshape(bd, td, N_HEADS, V_DIM),
                h_s, c_s, f_s)
        for o, val in zip(outs, vals):
            o.append(val)

    return (xp, xs, *[jnp.stack(o) for o in outs])
```

```python
import functools
import math

import jax
import jax.numpy as jnp
from jax import lax
from jax.experimental import pallas as pl
from jax.experimental.pallas import tpu as pltpu

F32 = jnp.float32
BF16 = jnp.bfloat16

D_MODEL = 1024
N_HEADS = 8
HEAD_DIM = 64
V_DIM = 128
N_MEM = 256
MEM_HEADS = 4
MEM_HEAD_DIM = 256
N_LRU_BLOCKS = 8
LRU_BLOCK = 128
LRU_CONV_W = 4
LRU_C = 8.0
D_FF = 3072
FFN_CONV_W = 3
RMS_EPS = 1e-6
NEG_INF = -1e30
LOG2_E = math.log2(math.e)
PAGE_SIZE = 128

N_F32_BLOCKS = 5
N_IN_BLOCKS = 9
ZB_Q, ZB_K, ZB_V, ZB_CQ = 0, 1, 2, 3

LANE = 128
SUBLANE = 8
VMEM_LIMIT = 56 * 1024 * 1024
IN_PROJ_TM = 256
ATTN_TQ = 512
ATTN_HP = 4
ATTN_CW = 256
ATTN_KH = 256
ATTN_AHEAD = 3


def _params(sem, vmem=VMEM_LIMIT):
    return pltpu.CompilerParams(dimension_semantics=sem, vmem_limit_bytes=vmem)


def _rms(x):
    return x * lax.rsqrt(jnp.mean(x * x, axis=-1, keepdims=True) + RMS_EPS)


def _dot(a, b):
    return jnp.dot(a, b, preferred_element_type=F32)


def _dot_nt(a, b):
    return lax.dot_general(a, b, (((1,), (1,)), ((), ())), preferred_element_type=F32)


def _dot_tn(a, b):
    return lax.dot_general(a, b, (((0,), (0,)), ((), ())), preferred_element_type=F32)


def _lam(q1, k1, q2, k2, lam_init):
    s1 = jnp.sum(q1 * k1, axis=-1, keepdims=True)
    s2 = jnp.sum(q2 * k2, axis=-1, keepdims=True)
    return jnp.exp(s1) - jnp.exp(s2) + lam_init


def _in_proj_kernel(x_ref, g_ref, w_ref, wkt_ref, zf_ref, zb_ref, k_ref, v_ref, h_ref, *,
                    k_transposed):
    h_ref[...] = (_rms(x_ref[...]) * g_ref[...]).astype(BF16)
    for n in range(N_F32_BLOCKS):
        zf_ref[:, n * D_MODEL:(n + 1) * D_MODEL] = _dot(h_ref[...], w_ref[n])
    for j in range(N_IN_BLOCKS - N_F32_BLOCKS):
        z = _dot(h_ref[...], w_ref[N_F32_BLOCKS + j])
        zb_ref[:, j * D_MODEL:(j + 1) * D_MODEL] = z.astype(zb_ref.dtype)
        if j == ZB_K and not k_transposed:
            k_ref[...] = z
        if j == ZB_V:
            v_ref[...] = z
    if k_transposed:
        k_ref[0] = _dot_nt(wkt_ref[...], h_ref[...])


def _in_proj(x, g, w_in, w_kt, zb_dtype, tm, seq_len, k_transposed):
    m = x.shape[0]
    nb = N_F32_BLOCKS
    nt = seq_len // tm
    if k_transposed:
        k_spec = pl.BlockSpec((1, D_MODEL, tm), lambda i: (i // nt, 0, i % nt))
        k_shape = jax.ShapeDtypeStruct((m // seq_len, D_MODEL, seq_len), F32)
    else:
        k_spec = pl.BlockSpec((tm, D_MODEL), lambda i: (i, 0))
        k_shape = jax.ShapeDtypeStruct((m, D_MODEL), F32)
    tok = lambda i: (i, 0)
    return pl.pallas_call(
        functools.partial(_in_proj_kernel, k_transposed=k_transposed),
        grid=(m // tm,),
        in_specs=[
            pl.BlockSpec((tm, D_MODEL), tok),
            pl.BlockSpec((1, D_MODEL), lambda i: (0, 0)),
            pl.BlockSpec((N_IN_BLOCKS, D_MODEL, D_MODEL), lambda i: (0, 0, 0),
                         pipeline_mode=pl.Buffered(1)),
            pl.BlockSpec((D_MODEL, D_MODEL), lambda i: (0, 0), pipeline_mode=pl.Buffered(1)),
        ],
        out_specs=[
            pl.BlockSpec((tm, nb * D_MODEL), tok),
            pl.BlockSpec((tm, (N_IN_BLOCKS - nb) * D_MODEL), tok),
            k_spec,
            pl.BlockSpec((tm, D_MODEL), tok),
        ],
        out_shape=[
            jax.ShapeDtypeStruct((m, nb * D_MODEL), F32),
            jax.ShapeDtypeStruct((m, (N_IN_BLOCKS - nb) * D_MODEL), zb_dtype),
            k_shape,
            jax.ShapeDtypeStruct((m, D_MODEL), F32),
        ],
        scratch_shapes=[pltpu.VMEM((tm, D_MODEL), BF16)],
        compiler_params=_params(("parallel",)),
        name="in_proj",
    )(x, g, w_in, w_kt)


def _lru_kernel(lx_ref, ly_ref, s0_ref, h0_ref, wc_ref, bc_ref, wrg_ref, ba_ref, bi_ref,
                lam_ref, out_ref, hN_ref, cN_ref, xp_ref, a_ref, u_ref, hc_ref, *, bb, tt):
    t = pl.program_id(1)
    W = LRU_CONV_W
    P = SUBLANE

    @pl.when(t == 0)
    def _():
        xp_ref[:, P - (W - 1):P, :] = s0_ref[...]
        hc_ref[...] = h0_ref[...]

    xp_ref[:, P:, :] = lx_ref[...]
    xc = xp_ref[:, P - 3:P - 3 + tt, :] * wc_ref[0:1, :]
    for j in range(1, W):
        xc = xc + xp_ref[:, P - 3 + j:P - 3 + j + tt, :] * wc_ref[j:j + 1, :]
    xc = xc + bc_ref[...]
    tail = xp_ref[:, tt + P - (W - 1):tt + P, :]
    xp_ref[:, P - (W - 1):P, :] = tail
    cN_ref[...] = tail

    x2 = xc.reshape(bb * tt, D_MODEL)
    xb = x2.astype(BF16)
    lin = [_dot(xb[:, n * LRU_BLOCK:(n + 1) * LRU_BLOCK], wrg_ref[n])
           for n in range(N_LRU_BLOCKS)]
    r_lin = jnp.concatenate([z[:, :LRU_BLOCK] for z in lin], axis=-1)
    i_lin = jnp.concatenate([z[:, LRU_BLOCK:] for z in lin], axis=-1)
    r = jax.nn.sigmoid(r_lin + ba_ref[...])
    gi = jax.nn.sigmoid(i_lin + bi_ref[...])
    neg_lam = -lam_ref[...]
    softplus = jnp.maximum(neg_lam, 0.0) + jnp.log1p(jnp.exp(-jnp.abs(neg_lam)))
    log_a = (-LRU_C * softplus) * r
    a = jnp.exp(log_a)
    u = jnp.sqrt(1.0 - a * a) * (gi * x2)

    g8 = (bb * tt) // SUBLANE
    a3 = a.reshape(g8, SUBLANE, D_MODEL)
    u3 = u.reshape(g8, SUBLANE, D_MODEL)
    row = lax.broadcasted_iota(jnp.int32, (g8, SUBLANE, D_MODEL), 1)
    for s in (1, 2, 4):
        a_sh = pltpu.roll(a3, s, 1)
        u_sh = pltpu.roll(u3, s, 1)
        ok = row >= s
        u3 = jnp.where(ok, a3 * u_sh + u3, u3)
        a3 = jnp.where(ok, a3 * a_sh, a3)
    a_ref[...] = a3.reshape(bb, tt, D_MODEL)
    u_ref[...] = u3.reshape(bb, tt, D_MODEL)

    def body(g, h):
        off = pl.multiple_of(g * SUBLANE, SUBLANE)
        hg = a_ref[:, pl.ds(off, SUBLANE), :] * h + u_ref[:, pl.ds(off, SUBLANE), :]
        u_ref[:, pl.ds(off, SUBLANE), :] = hg
        return hg[:, SUBLANE - 1:SUBLANE, :]

    h_last = lax.fori_loop(0, tt // SUBLANE, body, hc_ref[...])
    hc_ref[...] = h_last
    hN_ref[...] = h_last
    out_ref[...] = (jax.nn.gelu(ly_ref[...]) * u_ref[...]).astype(out_ref.dtype)


def _lru(zf3, s0, h0, wc, bc, wrg, ba, bi, lam, out_dtype, bb, tt):
    b, t, _ = zf3.shape
    kern = functools.partial(_lru_kernel, bb=bb, tt=tt)
    const2 = lambda i, j: (0, 0)
    return pl.pallas_call(
        kern,
        grid=(b // bb, t // tt),
        in_specs=[
            pl.BlockSpec((bb, tt, D_MODEL), lambda i, j: (i, j, 0)),
            pl.BlockSpec((bb, tt, D_MODEL), lambda i, j: (i, j, 1)),
            pl.BlockSpec((bb, LRU_CONV_W - 1, D_MODEL), lambda i, j: (i, 0, 0)),
            pl.BlockSpec((bb, 1, D_MODEL), lambda i, j: (i, 0, 0)),
            pl.BlockSpec((LRU_CONV_W, D_MODEL), const2),
            pl.BlockSpec((1, D_MODEL), const2),
            pl.BlockSpec((N_LRU_BLOCKS, LRU_BLOCK, 2 * LRU_BLOCK), lambda i, j: (0, 0, 0)),
            pl.BlockSpec((1, D_MODEL), const2),
            pl.BlockSpec((1, D_MODEL), const2),
            pl.BlockSpec((1, D_MODEL), const2),
        ],
        out_specs=[
            pl.BlockSpec((bb, tt, D_MODEL), lambda i, j: (i, j, 0)),
            pl.BlockSpec((bb, 1, D_MODEL), lambda i, j: (i, 0, 0)),
            pl.BlockSpec((bb, LRU_CONV_W - 1, D_MODEL), lambda i, j: (i, 0, 0)),
        ],
        out_shape=[
            jax.ShapeDtypeStruct((b, t, D_MODEL), out_dtype),
            jax.ShapeDtypeStruct((b, 1, D_MODEL), F32),
            jax.ShapeDtypeStruct((b, LRU_CONV_W - 1, D_MODEL), F32),
        ],
        scratch_shapes=[
            pltpu.VMEM((bb, tt + SUBLANE, D_MODEL), F32),
            pltpu.VMEM((bb, tt, D_MODEL), F32),
            pltpu.VMEM((bb, tt, D_MODEL), F32),
            pltpu.VMEM((bb, 1, D_MODEL), F32),
        ],
        compiler_params=_params(("parallel", "arbitrary")),
        name="rg_lru",
    )(zf3, zf3, s0, h0, wc, bc, wrg, ba, bi, lam)


def _attn_prompt_kernel(q_ref, k_ref, v_ref, lq1_ref, lk1_ref, lq2_ref, lk2_ref, gs_ref,
                        o_ref, qq_ref, m_ref, l_ref, acc_ref, *, tq, hp, lam_init):
    qi = pl.program_id(2)
    for hh in range(hp):
        q = q_ref[:, hh * V_DIM:(hh + 1) * V_DIM]
        lane = lax.broadcasted_iota(jnp.int32, q.shape, 1)
        zero = jnp.zeros_like(q)
        qq_ref[hh, :tq, :] = jnp.where(lane < HEAD_DIM, q, zero)
        qq_ref[hh, tq:, :] = jnp.where(lane >= HEAD_DIM, q, zero)
    m_ref[...] = jnp.full(m_ref.shape, NEG_INF, F32)
    l_ref[...] = jnp.zeros(l_ref.shape, F32)
    acc_ref[...] = jnp.zeros(acc_ref.shape, F32)

    def step(ki, diagonal):
        off = pl.multiple_of(ki * tq, tq)
        chunks = [(hh, c0) for hh in range(hp) for c0 in range(0, 2 * tq, ATTN_CW)]

        def n_keys(c0):
            return min(tq, c0 % tq + ATTN_CW) if diagonal else tq

        def scores(hh, c0):
            qc = qq_ref[hh, c0:c0 + ATTN_CW, :]
            s = jnp.concatenate(
                [_dot_nt(k_ref[pl.ds(off + r0, ATTN_KH), hh * V_DIM:(hh + 1) * V_DIM], qc)
                 for r0 in range(0, n_keys(c0), ATTN_KH)], axis=0)
            if diagonal:
                key = lax.broadcasted_iota(jnp.int32, s.shape, 0)
                qry = lax.broadcasted_iota(jnp.int32, s.shape, 1) + c0 % tq
                s = jnp.where(key <= qry, s, NEG_INF)
            return s

        def softmax(hh, c0, s):
            cols = slice(c0, c0 + ATTN_CW)
            m_old = m_ref[hh, :, cols]
            m_new = jnp.maximum(m_old, jnp.max(s, axis=0, keepdims=True))
            alpha = jnp.exp2(m_old - m_new)
            p = jnp.exp2(s - m_new)
            l_ref[hh, :, cols] = alpha * l_ref[hh, :, cols] + jnp.sum(p, axis=0, keepdims=True)
            m_ref[hh, :, cols] = m_new
            return alpha, p.astype(BF16)

        def values(hh, c0, alpha, p):
            cols = slice(c0, c0 + ATTN_CW)
            pv = None
            for r0 in range(0, n_keys(c0), ATTN_KH):
                vt = v_ref[pl.ds(off + r0, ATTN_KH), hh * V_DIM:(hh + 1) * V_DIM]
                part = _dot_tn(vt, p[r0:r0 + ATTN_KH, :])
                pv = part if pv is None else pv + part
            acc_ref[hh, :, cols] = alpha * acc_ref[hh, :, cols] + pv

        n = len(chunks)
        s_q = {i: scores(*chunks[i]) for i in range(min(ATTN_AHEAD, n))}
        p_q = {0: softmax(*chunks[0], s_q.pop(0))}
        for i in range(n):
            if i + ATTN_AHEAD < n:
                s_q[i + ATTN_AHEAD] = scores(*chunks[i + ATTN_AHEAD])
            if i + 1 < n:
                p_q[i + 1] = softmax(*chunks[i + 1], s_q.pop(i + 1))
            values(*chunks[i], *p_q.pop(i))

    def body(ki, c):
        step(ki, False)
        return c

    lax.fori_loop(0, qi, body, 0)
    step(qi, True)

    lam = _lam(lq1_ref[...], lk1_ref[...], lq2_ref[...], lk2_ref[...], lam_init)
    for hh in range(hp):
        o_t = acc_ref[hh] * (1.0 / l_ref[hh])
        o_t = o_t[:, :tq] - lam * o_t[:, tq:]
        o = o_t.T
        o_ref[:, hh * V_DIM:(hh + 1) * V_DIM] = (
            (_rms(o) * gs_ref[...]) * (1.0 - lam_init)).astype(o_ref.dtype)


def _attn_prompt(zb, lq1, lk1, lq2, lk2, gs, b, t, tq, hp, lam_init):
    m = zb.shape[0]
    nq = t // tq
    ng = N_HEADS // hp
    w = hp * V_DIM
    kern = functools.partial(_attn_prompt_kernel, tq=tq, hp=hp, lam_init=lam_init)
    vec = lambda i, h, j: (0, 0)
    return pl.pallas_call(
        kern,
        grid=(b, ng, nq),
        in_specs=[
            pl.BlockSpec((tq, w), lambda i, h, j: (i * nq + j, ZB_Q * ng + h)),
            pl.BlockSpec((t, w), lambda i, h, j: (i, ZB_K * ng + h), pipeline_mode=pl.Buffered(1)),
            pl.BlockSpec((t, w), lambda i, h, j: (i, ZB_V * ng + h), pipeline_mode=pl.Buffered(1)),
            pl.BlockSpec((1, HEAD_DIM), vec),
            pl.BlockSpec((1, HEAD_DIM), vec),
            pl.BlockSpec((1, HEAD_DIM), vec),
            pl.BlockSpec((1, HEAD_DIM), vec),
            pl.BlockSpec((1, V_DIM), vec),
        ],
        out_specs=pl.BlockSpec((tq, w), lambda i, h, j: (i * nq + j, h)),
        out_shape=jax.ShapeDtypeStruct((m, N_HEADS * V_DIM), BF16),
        scratch_shapes=[
            pltpu.VMEM((hp, 2 * tq, V_DIM), BF16),
            pltpu.VMEM((hp, 1, 2 * tq), F32),
            pltpu.VMEM((hp, 1, 2 * tq), F32),
            pltpu.VMEM((hp, V_DIM, 2 * tq), F32),
        ],
        compiler_params=_params(("parallel", "parallel", "arbitrary")),
        name="diff_attn_prompt",
    )(zb, zb, zb, lq1, lk1, lq2, lk2, gs)


def _attn_sample_kernel(pt_ref, q_ref, kn_ref, vn_ref, *rest, n_pages, tn, lam_init):
    del pt_ref
    kp_refs = rest[:n_pages]
    vp_refs = rest[n_pages:2 * n_pages]
    lq1_ref, lk1_ref, lq2_ref, lk2_ref, gs_ref, o_ref, s_ref = rest[2 * n_pages:]
    nrow = 2 * N_HEADS * tn
    n_past = n_pages * PAGE_SIZE
    HPB = 2
    nblk = N_HEADS // HPB
    rb = nrow // nblk
    fb = HPB * V_DIM

    q = q_ref[0]
    qt = jnp.concatenate([q] * (nrow // tn), axis=0)
    rj = lax.broadcasted_iota(jnp.int32, qt.shape, 0)
    cf = lax.broadcasted_iota(jnp.int32, qt.shape, 1)
    pair_j = rj >> (tn.bit_length() - 1)
    feat_blk = cf >> (HEAD_DIM.bit_length() - 1)
    qbd = jnp.where(feat_blk == pair_j, qt, 0.0)
    qblk = [qbd[g * rb:(g + 1) * rb, g * fb:(g + 1) * fb] for g in range(nblk)]

    for p in range(n_pages):
        for g in range(nblk):
            s_ref[g * rb:(g + 1) * rb, p * PAGE_SIZE:(p + 1) * PAGE_SIZE] = _dot(
                qblk[g], kp_refs[p][0, g * fb:(g + 1) * fb, :])
    s_new = _dot_nt(qbd, kn_ref[0])
    qry = lax.broadcasted_iota(jnp.int32, s_new.shape, 0) & (tn - 1)
    key = lax.broadcasted_iota(jnp.int32, s_new.shape, 1)
    s_new = jnp.where(key <= qry, s_new, NEG_INF)

    s = s_ref[...]
    mx = jnp.maximum(jnp.max(s, axis=-1, keepdims=True), jnp.max(s_new, axis=-1, keepdims=True))
    e = jnp.exp2(s - mx)
    e_new = jnp.exp2(s_new - mx)
    inv_l = 1.0 / (jnp.sum(e, axis=-1, keepdims=True) + jnp.sum(e_new, axis=-1, keepdims=True))
    s_ref[...] = e * inv_l
    p_new = e_new * inv_l

    lam = _lam(lq1_ref[...], lk1_ref[...], lq2_ref[...], lk2_ref[...], lam_init)
    outs = []
    for g in range(nblk):
        rows = slice(g * rb, (g + 1) * rb)
        acc = _dot(p_new[rows, :], vn_ref[0, :, g * fb:(g + 1) * fb])
        for p in range(n_pages):
            vg = jnp.concatenate(
                [vp_refs[p][0, pl.ds(g * HPB + hl, PAGE_SIZE, stride=N_HEADS), :]
                 for hl in range(HPB)], axis=-1)
            acc = acc + _dot(s_ref[rows, p * PAGE_SIZE:(p + 1) * PAGE_SIZE], vg)
        for hl in range(HPB):
            r0 = hl * 2 * tn
            cs = slice(hl * V_DIM, (hl + 1) * V_DIM)
            o = acc[r0:r0 + tn, cs] - lam * acc[r0 + tn:r0 + 2 * tn, cs]
            outs.append((_rms(o) * gs_ref[...]) * (1.0 - lam_init))
    o_ref[0] = jnp.concatenate(outs, axis=-1).astype(o_ref.dtype)


def _attn_sample(page_table, zb3, k3, v3, cache_kt, cache_v2, lq1, lk1, lq2, lk2, gs, lam_init):
    bd, tn, _ = k3.shape
    n_pages = page_table.shape[1]
    kern = functools.partial(_attn_sample_kernel, n_pages=n_pages, tn=tn, lam_init=lam_init)
    vec = lambda i, pt: (0, 0)

    def page_spec(p):
        return pl.BlockSpec((1, D_MODEL, PAGE_SIZE), lambda i, pt: (pt[i, p], 0, 0))

    grid_spec = pltpu.PrefetchScalarGridSpec(
        num_scalar_prefetch=1,
        grid=(bd,),
        in_specs=[
            pl.BlockSpec((1, tn, D_MODEL), lambda i, pt: (i, 0, ZB_Q)),
            pl.BlockSpec((1, tn, D_MODEL), lambda i, pt: (i, 0, 0)),
            pl.BlockSpec((1, tn, D_MODEL), lambda i, pt: (i, 0, 0)),
            *[page_spec(p) for p in range(n_pages)],
            *[page_spec(p) for p in range(n_pages)],
            pl.BlockSpec((1, HEAD_DIM), vec),
            pl.BlockSpec((1, HEAD_DIM), vec),
            pl.BlockSpec((1, HEAD_DIM), vec),
            pl.BlockSpec((1, HEAD_DIM), vec),
            pl.BlockSpec((1, V_DIM), vec),
        ],
        out_specs=pl.BlockSpec((1, tn, D_MODEL), lambda i, pt: (i, 0, 0)),
        scratch_shapes=[pltpu.VMEM((2 * N_HEADS * tn, n_pages * PAGE_SIZE), F32)],
    )
    return pl.pallas_call(
        kern,
        grid_spec=grid_spec,
        out_shape=jax.ShapeDtypeStruct((bd, tn, D_MODEL), F32),
        compiler_params=_params(("arbitrary",)),
        name="diff_attn_sample",
    )(page_table, zb3, k3, v3, *([cache_kt] * n_pages), *([cache_v2] * n_pages),
      lq1, lk1, lq2, lk2, gs)


def _mem_kv_kernel(x_ref, g_ref, w_ref, o_ref):
    h = (_rms(x_ref[...]) * g_ref[...]).astype(BF16)
    o_ref[0] = _dot(h, w_ref[...])


def _mem_kv(mem2, g, w):
    m = mem2.shape[0]
    n = w.shape[1]
    return pl.pallas_call(
        _mem_kv_kernel,
        grid=(n // D_MODEL,),
        in_specs=[
            pl.BlockSpec((m, D_MODEL), lambda j: (0, 0)),
            pl.BlockSpec((1, D_MODEL), lambda j: (0, 0)),
            pl.BlockSpec((D_MODEL, D_MODEL), lambda j: (0, j)),
        ],
        out_specs=pl.BlockSpec((1, m, D_MODEL), lambda j: (j, 0, 0)),
        out_shape=jax.ShapeDtypeStruct((n // D_MODEL, m, D_MODEL), F32),
        compiler_params=_params(("parallel",)),
        name="mem_kv",
    )(mem2, g, w)


MEM_ROW_GROUP = MEM_HEADS * (MEM_HEAD_DIM // LANE)


def _mem_attn_kernel(q_ref, mk_ref, mv_ref, o_ref, *, native):
    mxu_dtype = q_ref.dtype
    q = q_ref[0]

    def head(ref, h):
        if native:
            return jnp.concatenate(
                [ref[0, pl.ds(dc * MEM_HEADS + h, N_MEM, stride=MEM_ROW_GROUP), :]
                 for dc in range(MEM_HEAD_DIM // LANE)], axis=-1)
        return ref[0, :, h * MEM_HEAD_DIM:(h + 1) * MEM_HEAD_DIM].astype(mxu_dtype)

    scores = [_dot_nt(q[:, h * MEM_HEAD_DIM:(h + 1) * MEM_HEAD_DIM], head(mk_ref, h))
              for h in range(MEM_HEADS)]
    probs = []
    for s in scores:
        e = jnp.exp(s - jnp.max(s, axis=-1, keepdims=True))
        probs.append((e / jnp.sum(e, axis=-1, keepdims=True)).astype(mxu_dtype))
    outs = [_dot(probs[h], head(mv_ref, h)) for h in range(MEM_HEADS)]
    o_ref[0] = jnp.concatenate(outs, axis=-1).astype(o_ref.dtype)


def _mem_attn(zb3, mk3, mv3, out_dtype, tq, native):
    b, t, _ = zb3.shape
    mem_block = (1,) + mk3.shape[1:]
    return pl.pallas_call(
        functools.partial(_mem_attn_kernel, native=native),
        grid=(b, t // tq),
        in_specs=[
            pl.BlockSpec((1, tq, D_MODEL), lambda i, j: (i, j, ZB_CQ)),
            pl.BlockSpec(mem_block, lambda i, j: (i, 0, 0)),
            pl.BlockSpec(mem_block, lambda i, j: (i, 0, 0)),
        ],
        out_specs=pl.BlockSpec((1, tq, D_MODEL), lambda i, j: (i, j, 0)),
        out_shape=jax.ShapeDtypeStruct((b, t, D_MODEL), out_dtype),
        compiler_params=_params(("parallel", "arbitrary")),
        name="mem_attn",
    )(zb3, mk3, mv3)


def _merge_kernel(bl_ref, ba_ref, bm_ref, g0_ref, g1_ref, g2_ref, x_ref,
                  w0_ref, w1_ref, w2_ref, wo_ref, gpost_ref, gpre_ref, x1_ref, h2_ref):
    m = jax.nn.sigmoid(g0_ref[...]) * _dot(bl_ref[...].astype(BF16), w0_ref[...])
    m = m + jax.nn.sigmoid(g1_ref[...]) * _dot(ba_ref[...].astype(BF16), w1_ref[...])
    m = m + jax.nn.sigmoid(g2_ref[...]) * _dot(bm_ref[...].astype(BF16), w2_ref[...])
    y = _dot(m.astype(BF16), wo_ref[...])
    x1 = x_ref[...] + _rms(y) * gpost_ref[...]
    x1_ref[...] = x1
    h2_ref[...] = (_rms(x1) * gpre_ref[...]).astype(h2_ref.dtype)


def _merge(br_lru, br_attn, br_mem, zf, x, w0, w1, w2, wo, gpost, gpre, h2_dtype, tm):
    m = x.shape[0]
    tok = lambda i: (i, 0)
    cst = lambda i: (0, 0)
    wspec = pl.BlockSpec((D_MODEL, D_MODEL), cst)
    return pl.pallas_call(
        _merge_kernel,
        grid=(m // tm,),
        in_specs=[
            pl.BlockSpec((tm, D_MODEL), tok),
            pl.BlockSpec((tm, D_MODEL), tok),
            pl.BlockSpec((tm, D_MODEL), tok),
            pl.BlockSpec((tm, D_MODEL), lambda i: (i, 2)),
            pl.BlockSpec((tm, D_MODEL), lambda i: (i, 3)),
            pl.BlockSpec((tm, D_MODEL), lambda i: (i, 4)),
            pl.BlockSpec((tm, D_MODEL), tok),
            wspec, wspec, wspec, wspec,
            pl.BlockSpec((1, D_MODEL), cst),
            pl.BlockSpec((1, D_MODEL), cst),
        ],
        out_specs=[pl.BlockSpec((tm, D_MODEL), tok), pl.BlockSpec((tm, D_MODEL), tok)],
        out_shape=[jax.ShapeDtypeStruct((m, D_MODEL), F32),
                   jax.ShapeDtypeStruct((m, D_MODEL), h2_dtype)],
        compiler_params=_params(("parallel",)),
        name="merge_out_proj",
    )(br_lru, br_attn, br_mem, zf, zf, zf, x, w0, w1, w2, wo, gpost, gpre)


FFN_CHUNK = 512
FFN_AHEAD = 3


def _ffn_kernel(h_ref, x_ref, s0_ref, wup_ref, wc_ref, bc_ref, wdn_ref, g_ref,
                y_ref, cN_ref, up_ref, *, bb, tt):
    t = pl.program_id(1)
    W = FFN_CONV_W
    P = SUBLANE
    rows = bb * tt

    @pl.when(t == 0)
    def _():
        up_ref[:, P - (W - 1):P, :] = s0_ref[...]

    h = h_ref[...].reshape(rows, D_MODEL).astype(BF16)
    n_chunks = D_FF // FFN_CHUNK

    def chunk_cols(c):
        return [slice(base + c * FFN_CHUNK, base + (c + 1) * FFN_CHUNK) for base in (0, D_FF)]

    def up_project(c):
        for cs in chunk_cols(c):
            up_ref[:, P:, cs] = _dot(h, wup_ref[:, cs]).reshape(bb, tt, FFN_CHUNK)

    def activation(c):
        halves = []
        for cs in chunk_cols(c):
            uc = up_ref[:, P - 2:P - 2 + tt, cs] * wc_ref[0:1, cs]
            for j in range(1, W):
                uc = uc + up_ref[:, P - 2 + j:P - 2 + j + tt, cs] * wc_ref[j:j + 1, cs]
            halves.append((uc + bc_ref[:, cs]).reshape(rows, FFN_CHUNK))
        return (jax.nn.gelu(halves[0]) * halves[1]).astype(BF16)

    for c in range(min(FFN_AHEAD, n_chunks)):
        up_project(c)
    acc = jnp.zeros((rows, D_MODEL), F32)
    for c in range(n_chunks):
        if c + FFN_AHEAD < n_chunks:
            up_project(c + FFN_AHEAD)
        acc = acc + _dot(activation(c), wdn_ref[c * FFN_CHUNK:(c + 1) * FFN_CHUNK, :])

    tail = up_ref[:, tt + P - (W - 1):tt + P, :]
    up_ref[:, P - (W - 1):P, :] = tail
    cN_ref[...] = tail
    y = x_ref[...].reshape(rows, D_MODEL) + _rms(acc) * g_ref[...]
    y_ref[...] = y.reshape(bb, tt, D_MODEL)


def _ffn(h3, x3, s0, wup, wc, bc, wdn, g, bb, tt):
    b, t, _ = x3.shape
    kern = functools.partial(_ffn_kernel, bb=bb, tt=tt)
    cst = lambda i, j: (0, 0)
    once = pl.Buffered(1)
    return pl.pallas_call(
        kern,
        grid=(b // bb, t // tt),
        in_specs=[
            pl.BlockSpec((bb, tt, D_MODEL), lambda i, j: (i, j, 0)),
            pl.BlockSpec((bb, tt, D_MODEL), lambda i, j: (i, j, 0)),
            pl.BlockSpec((bb, FFN_CONV_W - 1, 2 * D_FF), lambda i, j: (i, 0, 0)),
            pl.BlockSpec((D_MODEL, 2 * D_FF), cst, pipeline_mode=once),
            pl.BlockSpec((FFN_CONV_W, 2 * D_FF), cst),
            pl.BlockSpec((1, 2 * D_FF), cst),
            pl.BlockSpec((D_FF, D_MODEL), cst, pipeline_mode=once),
            pl.BlockSpec((1, D_MODEL), cst),
        ],
        out_specs=[
            pl.BlockSpec((bb, tt, D_MODEL), lambda i, j: (i, j, 0)),
            pl.BlockSpec((bb, FFN_CONV_W - 1, 2 * D_FF), lambda i, j: (i, 0, 0)),
        ],
        out_shape=[
            jax.ShapeDtypeStruct((b, t, D_MODEL), F32),
            jax.ShapeDtypeStruct((b, FFN_CONV_W - 1, 2 * D_FF), F32),
        ],
        scratch_shapes=[pltpu.VMEM((bb, tt + SUBLANE, 2 * D_FF), F32)],
        compiler_params=_params(("parallel", "arbitrary")),
        name="conv_ffn",
    )(h3, x3, s0, wup, wc, bc, wdn, g)


def _layer(x3, attn_fn, mk3, mv3, lru_h0, lru_conv0, ffn_conv0, wts, *, prompt):
    b, t, _ = x3.shape
    m = b * t
    x2 = x3.reshape(m, D_MODEL)
    act_dtype = BF16 if prompt else F32
    tm = 512
    zf, zb, k, v = _in_proj(x2, wts["g_pre_mix"], wts["w_in"], wts["w_kt"], act_dtype,
                            IN_PROJ_TM, t if prompt else m, k_transposed=prompt)

    bb, tt = (1, 512) if prompt else (32, t)
    br_lru, lru_h, lru_conv = _lru(
        zf.reshape(b, t, -1), lru_conv0, lru_h0.reshape(b, 1, D_MODEL),
        wts["w_lru_conv"], wts["b_lru_conv"], wts["w_rg"], wts["b_rg_a"], wts["b_rg_i"],
        wts["lru_lambda"], act_dtype, bb, tt)

    br_attn = attn_fn(zb, k, v)
    br_mem = _mem_attn(zb.reshape(b, t, -1), mk3, mv3, act_dtype, 512 if prompt else t,
                        native=not prompt)

    x1, h2 = _merge(br_lru.reshape(m, D_MODEL), br_attn.reshape(m, D_MODEL),
                    br_mem.reshape(m, D_MODEL), zf, x2,
                    wts["w_br_lru"], wts["w_br_attn"], wts["w_br_mem"], wts["w_out"],
                    wts["g_post_mix"], wts["g_pre_ffn"], act_dtype, tm)

    bb, tt = (1, 256) if prompt else (16, t)
    y, ffn_conv = _ffn(h2.reshape(b, t, D_MODEL), x1.reshape(b, t, D_MODEL), ffn_conv0,
                       wts["w_up"], wts["w_ffn_conv"], wts["b_ffn_conv"], wts["w_down"],
                       wts["g_post_ffn"], bb, tt)
    return y, k, v, lru_h.reshape(b, D_MODEL), lru_conv, ffn_conv


def _lambda_init(layer_idx):
    return 0.8 - 0.6 * math.exp(-0.3 * layer_idx)


def kernel(x_prompt, x_sample, mem_prompt, cache_k, cache_v, page_table, cache_mem_k, cache_mem_v, state_lru_h, state_lru_conv, state_ffn_conv, g_pre_mix, w_in, w_lru_conv, b_lru_conv, w_rg_a, b_rg_a, w_rg_i, b_rg_i, lru_lambda, lambda_q1, lambda_k1, lambda_q2, lambda_k2, g_subln, g_mem, w_mem_kv, w_br_lru, w_br_attn, w_br_mem, w_out, g_post_mix, g_pre_ffn, w_up, w_ffn_conv, b_ffn_conv, w_down, g_post_ffn):
    depth = w_in.shape[0]
    bp, tp, _ = x_prompt.shape
    bd, td, _ = x_sample.shape
    xp, xs = x_prompt, x_sample
    outs = [[] for _ in range(12)]
    row = lambda a: a.reshape(1, -1)

    for l in range(depth):
        lam_init = _lambda_init(l)
        wi = w_in[l]
        blk = lambda i: wi[:, i * D_MODEL:(i + 1) * D_MODEL]
        w_in_p = jnp.stack(
            [blk(0), blk(1), blk(6), blk(7), blk(8),
             blk(2) * (LOG2_E / math.sqrt(HEAD_DIM)), blk(3), blk(4),
             blk(5) * (1.0 / math.sqrt(MEM_HEAD_DIM))]).astype(BF16)
        wts = dict(
            g_pre_mix=row(g_pre_mix[l]), w_in=w_in_p, w_kt=blk(3).T.astype(BF16),
            w_lru_conv=w_lru_conv[l], b_lru_conv=row(b_lru_conv[l]),
            w_rg=jnp.concatenate([w_rg_a[l], w_rg_i[l]], axis=-1).astype(BF16),
            b_rg_a=row(b_rg_a[l]), b_rg_i=row(b_rg_i[l]), lru_lambda=row(lru_lambda[l]),
            w_br_lru=w_br_lru[l].astype(BF16), w_br_attn=w_br_attn[l].astype(BF16),
            w_br_mem=w_br_mem[l].astype(BF16), w_out=w_out[l].astype(BF16),
            g_post_mix=row(g_post_mix[l]), g_pre_ffn=row(g_pre_ffn[l]),
            w_up=w_up[l].astype(BF16), w_ffn_conv=w_ffn_conv[l], b_ffn_conv=row(b_ffn_conv[l]),
            w_down=w_down[l].astype(BF16), g_post_ffn=row(g_post_ffn[l]),
        )
        lams = (row(lambda_q1[l]), row(lambda_k1[l]), row(lambda_q2[l]), row(lambda_k2[l]))
        gs = row(g_subln[l])

        mkv = _mem_kv(mem_prompt.reshape(bp * N_MEM, D_MODEL), row(g_mem[l]),
                      w_mem_kv[l].astype(BF16))
        mk_p = mkv[0].reshape(bp, N_MEM, D_MODEL)
        mv_p = mkv[1].reshape(bp, N_MEM, D_MODEL)
        attn_p = lambda zb, k, v: _attn_prompt(zb, *lams, gs, bp, tp, ATTN_TQ, ATTN_HP, lam_init)
        xp, k_p, v_p, h_p, c_p, f_p = _layer(
            xp, attn_p, mk_p, mv_p,
            jnp.zeros((bp, D_MODEL), F32),
            jnp.zeros((bp, LRU_CONV_W - 1, D_MODEL), F32),
            jnp.zeros((bp, FFN_CONV_W - 1, 2 * D_FF), F32),
            wts, prompt=True)

        ck = cache_k[l].transpose(0, 2, 3, 4, 1).reshape(-1, D_MODEL, PAGE_SIZE)
        cv = cache_v[l].reshape(-1, PAGE_SIZE * N_HEADS, V_DIM)
        mem_view = lambda c: c.reshape(bd, N_MEM, MEM_HEADS, MEM_HEAD_DIM // LANE, LANE).transpose(
            0, 1, 3, 2, 4).reshape(bd, N_MEM * MEM_ROW_GROUP, LANE)
        attn_s = lambda zb, k, v: _attn_sample(
            page_table, zb.reshape(bd, td, -1), k.reshape(bd, td, D_MODEL),
            v.reshape(bd, td, D_MODEL), ck, cv, *lams, gs, lam_init)
        xs, k_s, v_s, h_s, c_s, f_s = _layer(
            xs, attn_s, mem_view(cache_mem_k[l]), mem_view(cache_mem_v[l]),
            state_lru_h[l], state_lru_conv[l], state_ffn_conv[l], wts, prompt=False)

        k_p = k_p.reshape(bp, N_HEADS, 2, HEAD_DIM, tp).transpose(0, 4, 1, 2, 3)
        vals = (k_p, v_p.reshape(bp, tp, N_HEADS, V_DIM),
                mk_p.reshape(bp, N_MEM, MEM_HEADS, MEM_HEAD_DIM),
                mv_p.reshape(bp, N_MEM, MEM_HEADS, MEM_HEAD_DIM), h_p, c_p, f_p,
                k_s.reshape(bd, td, N_HEADS, 2, HEAD_DIM), v_s.reshape(bd, td, N_HEADS, V_DIM),
                h_s, c_s, f_s)
        for o, val in zip(outs, vals):
            o.append(val)

    return (xp, xs, *[jnp.stack(o) for o in outs])
```

```python
import functools
import math

import jax
import jax.numpy as jnp
from jax import lax
from jax.experimental import pallas as pl
from jax.experimental.pallas import tpu as pltpu

F32 = jnp.float32
BF16 = jnp.bfloat16

D_MODEL = 1024
N_HEADS = 8
HEAD_DIM = 64
V_DIM = 128
N_MEM = 256
MEM_HEADS = 4
MEM_HEAD_DIM = 256
N_LRU_BLOCKS = 8
LRU_BLOCK = 128
LRU_CONV_W = 4
LRU_C = 8.0
D_FF = 3072
FFN_CONV_W = 3
RMS_EPS = 1e-6
NEG_INF = -1e30
LOG2_E = math.log2(math.e)
PAGE_SIZE = 128

N_F32_BLOCKS = 5
N_IN_BLOCKS = 9
ZB_Q, ZB_K, ZB_V, ZB_CQ = 0, 1, 2, 3

LANE = 128
SUBLANE = 8
VMEM_LIMIT = 56 * 1024 * 1024
IN_PROJ_TM = 256
ATTN_TQ = 512
ATTN_HP = 4
ATTN_CW = 256
ATTN_KH = 256
ATTN_AHEAD = 3


def _params(sem, vmem=VMEM_LIMIT):
    return pltpu.CompilerParams(dimension_semantics=sem, vmem_limit_bytes=vmem)


def _rms(x):
    return x * lax.rsqrt(jnp.mean(x * x, axis=-1, keepdims=True) + RMS_EPS)


def _dot(a, b):
    return jnp.dot(a, b, preferred_element_type=F32)


def _dot_nt(a, b):
    return lax.dot_general(a, b, (((1,), (1,)), ((), ())), preferred_element_type=F32)


def _dot_tn(a, b):
    return lax.dot_general(a, b, (((0,), (0,)), ((), ())), preferred_element_type=F32)


def _lam(q1, k1, q2, k2, lam_init):
    s1 = jnp.sum(q1 * k1, axis=-1, keepdims=True)
    s2 = jnp.sum(q2 * k2, axis=-1, keepdims=True)
    return jnp.exp(s1) - jnp.exp(s2) + lam_init


def _in_proj_kernel(x_ref, g_ref, w_ref, wkt_ref, zf_ref, zb_ref, k_ref, v_ref, h_ref, *,
                    k_transposed):
    h_ref[...] = (_rms(x_ref[...]) * g_ref[...]).astype(BF16)
    for n in range(N_F32_BLOCKS):
        zf_ref[:, n * D_MODEL:(n + 1) * D_MODEL] = _dot(h_ref[...], w_ref[n])
    for j in range(N_IN_BLOCKS - N_F32_BLOCKS):
        z = _dot(h_ref[...], w_ref[N_F32_BLOCKS + j])
        zb_ref[:, j * D_MODEL:(j + 1) * D_MODEL] = z.astype(zb_ref.dtype)
        if j == ZB_K and not k_transposed:
            k_ref[...] = z
        if j == ZB_V:
            v_ref[...] = z
    if k_transposed:
        k_ref[0] = _dot_nt(wkt_ref[...], h_ref[...])


def _in_proj(x, g, w_in, w_kt, zb_dtype, tm, seq_len, k_transposed):
    m = x.shape[0]
    nb = N_F32_BLOCKS
    nt = seq_len // tm
    if k_transposed:
        k_spec = pl.BlockSpec((1, D_MODEL, tm), lambda i: (i // nt, 0, i % nt))
        k_shape = jax.ShapeDtypeStruct((m // seq_len, D_MODEL, seq_len), F32)
    else:
        k_spec = pl.BlockSpec((tm, D_MODEL), lambda i: (i, 0))
        k_shape = jax.ShapeDtypeStruct((m, D_MODEL), F32)
    tok = lambda i: (i, 0)
    return pl.pallas_call(
        functools.partial(_in_proj_kernel, k_transposed=k_transposed),
        grid=(m // tm,),
        in_specs=[
            pl.BlockSpec((tm, D_MODEL), tok),
            pl.BlockSpec((1, D_MODEL), lambda i: (0, 0)),
            pl.BlockSpec((N_IN_BLOCKS, D_MODEL, D_MODEL), lambda i: (0, 0, 0),
                         pipeline_mode=pl.Buffered(1)),
            pl.BlockSpec((D_MODEL, D_MODEL), lambda i: (0, 0), pipeline_mode=pl.Buffered(1)),
        ],
        out_specs=[
            pl.BlockSpec((tm, nb * D_MODEL), tok),
            pl.BlockSpec((tm, (N_IN_BLOCKS - nb) * D_MODEL), tok),
            k_spec,
            pl.BlockSpec((tm, D_MODEL), tok),
        ],
        out_shape=[
            jax.ShapeDtypeStruct((m, nb * D_MODEL), F32),
            jax.ShapeDtypeStruct((m, (N_IN_BLOCKS - nb) * D_MODEL), zb_dtype),
            k_shape,
            jax.ShapeDtypeStruct((m, D_MODEL), F32),
        ],
        scratch_shapes=[pltpu.VMEM((tm, D_MODEL), BF16)],
        compiler_params=_params(("parallel",)),
        name="in_proj",
    )(x, g, w_in, w_kt)


def _lru_kernel(lx_ref, ly_ref, s0_ref, h0_ref, wc_ref, bc_ref, wrg_ref, ba_ref, bi_ref,
                lam_ref, out_ref, hN_ref, cN_ref, xp_ref, a_ref, u_ref, hc_ref, *, bb, tt):
    t = pl.program_id(1)
    W = LRU_CONV_W
    P = SUBLANE

    @pl.when(t == 0)
    def _():
        xp_ref[:, P - (W - 1):P, :] = s0_ref[...]
        hc_ref[...] = h0_ref[...]

    xp_ref[:, P:, :] = lx_ref[...]
    xc = xp_ref[:, P - 3:P - 3 + tt, :] * wc_ref[0:1, :]
    for j in range(1, W):
        xc = xc + xp_ref[:, P - 3 + j:P - 3 + j + tt, :] * wc_ref[j:j + 1, :]
    xc = xc + bc_ref[...]
    tail = xp_ref[:, tt + P - (W - 1):tt + P, :]
    xp_ref[:, P - (W - 1):P, :] = tail
    cN_ref[...] = tail

    x2 = xc.reshape(bb * tt, D_MODEL)
    xb = x2.astype(BF16)
    lin = [_dot(xb[:, n * LRU_BLOCK:(n + 1) * LRU_BLOCK], wrg_ref[n])
           for n in range(N_LRU_BLOCKS)]
    r_lin = jnp.concatenate([z[:, :LRU_BLOCK] for z in lin], axis=-1)
    i_lin = jnp.concatenate([z[:, LRU_BLOCK:] for z in lin], axis=-1)
    r = jax.nn.sigmoid(r_lin + ba_ref[...])
    gi = jax.nn.sigmoid(i_lin + bi_ref[...])
    neg_lam = -lam_ref[...]
    softplus = jnp.maximum(neg_lam, 0.0) + jnp.log1p(jnp.exp(-jnp.abs(neg_lam)))
    log_a = (-LRU_C * softplus) * r
    a = jnp.exp(log_a)
    u = jnp.sqrt(1.0 - a * a) * (gi * x2)

    g8 = (bb * tt) // SUBLANE
    a3 = a.reshape(g8, SUBLANE, D_MODEL)
    u3 = u.reshape(g8, SUBLANE, D_MODEL)
    row = lax.broadcasted_iota(jnp.int32, (g8, SUBLANE, D_MODEL), 1)
    for s in (1, 2, 4):
        a_sh = pltpu.roll(a3, s, 1)
        u_sh = pltpu.roll(u3, s, 1)
        ok = row >= s
        u3 = jnp.where(ok, a3 * u_sh + u3, u3)
        a3 = jnp.where(ok, a3 * a_sh, a3)
    a_ref[...] = a3.reshape(bb, tt, D_MODEL)
    u_ref[...] = u3.reshape(bb, tt, D_MODEL)

    def body(g, h):
        off = pl.multiple_of(g * SUBLANE, SUBLANE)
        hg = a_ref[:, pl.ds(off, SUBLANE), :] * h + u_ref[:, pl.ds(off, SUBLANE), :]
        u_ref[:, pl.ds(off, SUBLANE), :] = hg
        return hg[:, SUBLANE - 1:SUBLANE, :]

    h_last = lax.fori_loop(0, tt // SUBLANE, body, hc_ref[...])
    hc_ref[...] = h_last
    hN_ref[...] = h_last
    out_ref[...] = (jax.nn.gelu(ly_ref[...]) * u_ref[...]).astype(out_ref.dtype)


def _lru(zf3, s0, h0, wc, bc, wrg, ba, bi, lam, out_dtype, bb, tt):
    b, t, _ = zf3.shape
    kern = functools.partial(_lru_kernel, bb=bb, tt=tt)
    const2 = lambda i, j: (0, 0)
    return pl.pallas_call(
        kern,
        grid=(b // bb, t // tt),
        in_specs=[
            pl.BlockSpec((bb, tt, D_MODEL), lambda i, j: (i, j, 0)),
            pl.BlockSpec((bb, tt, D_MODEL), lambda i, j: (i, j, 1)),
            pl.BlockSpec((bb, LRU_CONV_W - 1, D_MODEL), lambda i, j: (i, 0, 0)),
            pl.BlockSpec((bb, 1, D_MODEL), lambda i, j: (i, 0, 0)),
            pl.BlockSpec((LRU_CONV_W, D_MODEL), const2),
            pl.BlockSpec((1, D_MODEL), const2),
            pl.BlockSpec((N_LRU_BLOCKS, LRU_BLOCK, 2 * LRU_BLOCK), lambda i, j: (0, 0, 0)),
            pl.BlockSpec((1, D_MODEL), const2),
            pl.BlockSpec((1, D_MODEL), const2),
            pl.BlockSpec((1, D_MODEL), const2),
        ],
        out_specs=[
            pl.BlockSpec((bb, tt, D_MODEL), lambda i, j: (i, j, 0)),
            pl.BlockSpec((bb, 1, D_MODEL), lambda i, j: (i, 0, 0)),
            pl.BlockSpec((bb, LRU_CONV_W - 1, D_MODEL), lambda i, j: (i, 0, 0)),
        ],
        out_shape=[
            jax.ShapeDtypeStruct((b, t, D_MODEL), out_dtype),
            jax.ShapeDtypeStruct((b, 1, D_MODEL), F32),
            jax.ShapeDtypeStruct((b, LRU_CONV_W - 1, D_MODEL), F32),
        ],
        scratch_shapes=[
            pltpu.VMEM((bb, tt + SUBLANE, D_MODEL), F32),
            pltpu.VMEM((bb, tt, D_MODEL), F32),
            pltpu.VMEM((bb, tt, D_MODEL), F32),
            pltpu.VMEM((bb, 1, D_MODEL), F32),
        ],
        compiler_params=_params(("parallel", "arbitrary")),
        name="rg_lru",
    )(zf3, zf3, s0, h0, wc, bc, wrg, ba, bi, lam)


def _attn_prompt_kernel(q_ref, k_ref, v_ref, lq1_ref, lk1_ref, lq2_ref, lk2_ref, gs_ref,
                        o_ref, qq_ref, m_ref, l_ref, acc_ref, *, tq, hp, lam_init):
    qi = pl.program_id(2)
    for hh in range(hp):
        q = q_ref[:, hh * V_DIM:(hh + 1) * V_DIM]
        lane = lax.broadcasted_iota(jnp.int32, q.shape, 1)
        zero = jnp.zeros_like(q)
        qq_ref[hh, :tq, :] = jnp.where(lane < HEAD_DIM, q, zero)
        qq_ref[hh, tq:, :] = jnp.where(lane >= HEAD_DIM, q, zero)
    m_ref[...] = jnp.full(m_ref.shape, NEG_INF, F32)
    l_ref[...] = jnp.zeros(l_ref.shape, F32)
    acc_ref[...] = jnp.zeros(acc_ref.shape, F32)

    def run(tiles):
        offs = [pl.multiple_of(ki * tq, tq) for ki, _ in tiles]
        chunks = [(ti, hh, c0) for ti in range(len(tiles)) for hh in range(hp)
                  for c0 in range(0, 2 * tq, ATTN_CW)]

        def n_keys(ti, c0):
            return min(tq, c0 % tq + ATTN_CW) if tiles[ti][1] else tq

        def scores(ti, hh, c0):
            qc = qq_ref[hh, c0:c0 + ATTN_CW, :]
            s = jnp.concatenate(
                [_dot_nt(k_ref[pl.ds(offs[ti] + r0, ATTN_KH), hh * V_DIM:(hh + 1) * V_DIM], qc)
                 for r0 in range(0, n_keys(ti, c0), ATTN_KH)], axis=0)
            if tiles[ti][1]:
                key = lax.broadcasted_iota(jnp.int32, s.shape, 0)
                qry = lax.broadcasted_iota(jnp.int32, s.shape, 1) + c0 % tq
                s = jnp.where(key <= qry, s, NEG_INF)
            return s

        def softmax(ti, hh, c0, s):
            cols = slice(c0, c0 + ATTN_CW)
            m_old = m_ref[hh, :, cols]
            m_new = jnp.maximum(m_old, jnp.max(s, axis=0, keepdims=True))
            alpha = jnp.exp2(m_old - m_new)
            p = jnp.exp2(s - m_new)
            l_ref[hh, :, cols] = alpha * l_ref[hh, :, cols] + jnp.sum(p, axis=0, keepdims=True)
            m_ref[hh, :, cols] = m_new
            return alpha, p.astype(BF16)

        def values(ti, hh, c0, alpha, p):
            cols = slice(c0, c0 + ATTN_CW)
            pv = None
            for r0 in range(0, n_keys(ti, c0), ATTN_KH):
                vt = v_ref[pl.ds(offs[ti] + r0, ATTN_KH), hh * V_DIM:(hh + 1) * V_DIM]
                part = _dot_tn(vt, p[r0:r0 + ATTN_KH, :])
                pv = part if pv is None else pv + part
            acc_ref[hh, :, cols] = alpha * acc_ref[hh, :, cols] + pv

        n = len(chunks)
        s_q = {i: scores(*chunks[i]) for i in range(min(ATTN_AHEAD, n))}
        p_q = {0: softmax(*chunks[0], s_q.pop(0))}
        for i in range(n):
            if i + ATTN_AHEAD < n:
                s_q[i + ATTN_AHEAD] = scores(*chunks[i + ATTN_AHEAD])
            if i + 1 < n:
                p_q[i + 1] = softmax(*chunks[i + 1], s_q.pop(i + 1))
            values(*chunks[i], *p_q.pop(i))

    def body(j, c):
        run([(2 * j, False), (2 * j + 1, False)])
        return c

    lax.fori_loop(0, qi // 2, body, 0)

    @pl.when(qi % 2 == 1)
    def _():
        run([(qi - 1, False), (qi, True)])

    @pl.when(qi % 2 == 0)
    def _():
        run([(qi, True)])


    lam = _lam(lq1_ref[...], lk1_ref[...], lq2_ref[...], lk2_ref[...], lam_init)
    for hh in range(hp):
        o_t = acc_ref[hh] * (1.0 / l_ref[hh])
        o_t = o_t[:, :tq] - lam * o_t[:, tq:]
        o = o_t.T
        o_ref[:, hh * V_DIM:(hh + 1) * V_DIM] = (
            (_rms(o) * gs_ref[...]) * (1.0 - lam_init)).astype(o_ref.dtype)


def _attn_prompt(zb, lq1, lk1, lq2, lk2, gs, b, t, tq, hp, lam_init):
    m = zb.shape[0]
    nq = t // tq
    ng = N_HEADS // hp
    w = hp * V_DIM
    kern = functools.partial(_attn_prompt_kernel, tq=tq, hp=hp, lam_init=lam_init)
    vec = lambda i, h, j: (0, 0)
    return pl.pallas_call(
        kern,
        grid=(b, ng, nq),
        in_specs=[
            pl.BlockSpec((tq, w), lambda i, h, j: (i * nq + j, ZB_Q * ng + h)),
            pl.BlockSpec((t, w), lambda i, h, j: (i, ZB_K * ng + h), pipeline_mode=pl.Buffered(1)),
            pl.BlockSpec((t, w), lambda i, h, j: (i, ZB_V * ng + h), pipeline_mode=pl.Buffered(1)),
            pl.BlockSpec((1, HEAD_DIM), vec),
            pl.BlockSpec((1, HEAD_DIM), vec),
            pl.BlockSpec((1, HEAD_DIM), vec),
            pl.BlockSpec((1, HEAD_DIM), vec),
            pl.BlockSpec((1, V_DIM), vec),
        ],
        out_specs=pl.BlockSpec((tq, w), lambda i, h, j: (i * nq + j, h)),
        out_shape=jax.ShapeDtypeStruct((m, N_HEADS * V_DIM), BF16),
        scratch_shapes=[
            pltpu.VMEM((hp, 2 * tq, V_DIM), BF16),
            pltpu.VMEM((hp, 1, 2 * tq), F32),
            pltpu.VMEM((hp, 1, 2 * tq), F32),
            pltpu.VMEM((hp, V_DIM, 2 * tq), F32),
        ],
        compiler_params=_params(("parallel", "parallel", "arbitrary")),
        name="diff_attn_prompt",
    )(zb, zb, zb, lq1, lk1, lq2, lk2, gs)


def _attn_sample_kernel(pt_ref, q_ref, kn_ref, vn_ref, *rest, n_pages, tn, lam_init):
    del pt_ref
    kp_refs = rest[:n_pages]
    vp_refs = rest[n_pages:2 * n_pages]
    lq1_ref, lk1_ref, lq2_ref, lk2_ref, gs_ref, o_ref, s_ref = rest[2 * n_pages:]
    nrow = 2 * N_HEADS * tn
    n_past = n_pages * PAGE_SIZE
    HPB = 2
    nblk = N_HEADS // HPB
    rb = nrow // nblk
    fb = HPB * V_DIM

    q = q_ref[0]
    qt = jnp.concatenate([q] * (nrow // tn), axis=0)
    rj = lax.broadcasted_iota(jnp.int32, qt.shape, 0)
    cf = lax.broadcasted_iota(jnp.int32, qt.shape, 1)
    pair_j = rj >> (tn.bit_length() - 1)
    feat_blk = cf >> (HEAD_DIM.bit_length() - 1)
    qbd = jnp.where(feat_blk == pair_j, qt, 0.0)
    qblk = [qbd[g * rb:(g + 1) * rb, g * fb:(g + 1) * fb] for g in range(nblk)]

    for p in range(n_pages):
        for g in range(nblk):
            s_ref[g * rb:(g + 1) * rb, p * PAGE_SIZE:(p + 1) * PAGE_SIZE] = _dot(
                qblk[g], kp_refs[p][0, g * fb:(g + 1) * fb, :])
    s_new = _dot_nt(qbd, kn_ref[0])
    qry = lax.broadcasted_iota(jnp.int32, s_new.shape, 0) & (tn - 1)
    key = lax.broadcasted_iota(jnp.int32, s_new.shape, 1)
    s_new = jnp.where(key <= qry, s_new, NEG_INF)

    s = s_ref[...]
    mx = jnp.maximum(jnp.max(s, axis=-1, keepdims=True), jnp.max(s_new, axis=-1, keepdims=True))
    e = jnp.exp2(s - mx)
    e_new = jnp.exp2(s_new - mx)
    inv_l = 1.0 / (jnp.sum(e, axis=-1, keepdims=True) + jnp.sum(e_new, axis=-1, keepdims=True))
    s_ref[...] = e * inv_l
    p_new = e_new * inv_l

    lam = _lam(lq1_ref[...], lk1_ref[...], lq2_ref[...], lk2_ref[...], lam_init)
    outs = []
    for g in range(nblk):
        rows = slice(g * rb, (g + 1) * rb)
        acc = _dot(p_new[rows, :], vn_ref[0, :, g * fb:(g + 1) * fb])
        for p in range(n_pages):
            vg = jnp.concatenate(
                [vp_refs[p][0, pl.ds(g * HPB + hl, PAGE_SIZE, stride=N_HEADS), :]
                 for hl in range(HPB)], axis=-1)
            acc = acc + _dot(s_ref[rows, p * PAGE_SIZE:(p + 1) * PAGE_SIZE], vg)
        for hl in range(HPB):
            r0 = hl * 2 * tn
            cs = slice(hl * V_DIM, (hl + 1) * V_DIM)
            o = acc[r0:r0 + tn, cs] - lam * acc[r0 + tn:r0 + 2 * tn, cs]
            outs.append((_rms(o) * gs_ref[...]) * (1.0 - lam_init))
    o_ref[0] = jnp.concatenate(outs, axis=-1).astype(o_ref.dtype)


def _attn_sample(page_table, zb3, k3, v3, cache_kt, cache_v2, lq1, lk1, lq2, lk2, gs, lam_init):
    bd, tn, _ = k3.shape
    n_pages = page_table.shape[1]
    kern = functools.partial(_attn_sample_kernel, n_pages=n_pages, tn=tn, lam_init=lam_init)
    vec = lambda i, pt: (0, 0)

    def page_spec(p):
        return pl.BlockSpec((1, D_MODEL, PAGE_SIZE), lambda i, pt: (pt[i, p], 0, 0))

    grid_spec = pltpu.PrefetchScalarGridSpec(
        num_scalar_prefetch=1,
        grid=(bd,),
        in_specs=[
            pl.BlockSpec((1, tn, D_MODEL), lambda i, pt: (i, 0, ZB_Q)),
            pl.BlockSpec((1, tn, D_MODEL), lambda i, pt: (i, 0, 0)),
            pl.BlockSpec((1, tn, D_MODEL), lambda i, pt: (i, 0, 0)),
            *[page_spec(p) for p in range(n_pages)],
            *[page_spec(p) for p in range(n_pages)],
            pl.BlockSpec((1, HEAD_DIM), vec),
            pl.BlockSpec((1, HEAD_DIM), vec),
            pl.BlockSpec((1, HEAD_DIM), vec),
            pl.BlockSpec((1, HEAD_DIM), vec),
            pl.BlockSpec((1, V_DIM), vec),
        ],
        out_specs=pl.BlockSpec((1, tn, D_MODEL), lambda i, pt: (i, 0, 0)),
        scratch_shapes=[pltpu.VMEM((2 * N_HEADS * tn, n_pages * PAGE_SIZE), F32)],
    )
    return pl.pallas_call(
        kern,
        grid_spec=grid_spec,
        out_shape=jax.ShapeDtypeStruct((bd, tn, D_MODEL), F32),
        compiler_params=_params(("arbitrary",)),
        name="diff_attn_sample",
    )(page_table, zb3, k3, v3, *([cache_kt] * n_pages), *([cache_v2] * n_pages),
      lq1, lk1, lq2, lk2, gs)


def _mem_kv_kernel(x_ref, g_ref, w_ref, o_ref):
    h = (_rms(x_ref[...]) * g_ref[...]).astype(BF16)
    o_ref[0] = _dot(h, w_ref[...])


def _mem_kv(mem2, g, w):
    m = mem2.shape[0]
    n = w.shape[1]
    return pl.pallas_call(
        _mem_kv_kernel,
        grid=(n // D_MODEL,),
        in_specs=[
            pl.BlockSpec((m, D_MODEL), lambda j: (0, 0)),
            pl.BlockSpec((1, D_MODEL), lambda j: (0, 0)),
            pl.BlockSpec((D_MODEL, D_MODEL), lambda j: (0, j)),
        ],
        out_specs=pl.BlockSpec((1, m, D_MODEL), lambda j: (j, 0, 0)),
        out_shape=jax.ShapeDtypeStruct((n // D_MODEL, m, D_MODEL), F32),
        compiler_params=_params(("parallel",)),
        name="mem_kv",
    )(mem2, g, w)


MEM_ROW_GROUP = MEM_HEADS * (MEM_HEAD_DIM // LANE)


def _mem_attn_kernel(q_ref, mk_ref, mv_ref, o_ref, *, native, nb):
    mxu_dtype = q_ref.dtype

    def head(ref, i, h):
        if native:
            return jnp.concatenate(
                [ref[i, pl.ds(dc * MEM_HEADS + h, N_MEM, stride=MEM_ROW_GROUP), :]
                 for dc in range(MEM_HEAD_DIM // LANE)], axis=-1)
        return ref[i, :, h * MEM_HEAD_DIM:(h + 1) * MEM_HEAD_DIM].astype(mxu_dtype)

    pairs = [(i, h) for i in range(nb) for h in range(MEM_HEADS)]
    scores = [_dot_nt(q_ref[i, :, h * MEM_HEAD_DIM:(h + 1) * MEM_HEAD_DIM], head(mk_ref, i, h))
              for i, h in pairs]
    probs = []
    for s in scores:
        e = jnp.exp(s - jnp.max(s, axis=-1, keepdims=True))
        probs.append((e / jnp.sum(e, axis=-1, keepdims=True)).astype(mxu_dtype))
    outs = [_dot(p, head(mv_ref, i, h)) for p, (i, h) in zip(probs, pairs)]
    for i in range(nb):
        o_ref[i] = jnp.concatenate(outs[i * MEM_HEADS:(i + 1) * MEM_HEADS],
                                   axis=-1).astype(o_ref.dtype)


def _mem_attn(zb3, mk3, mv3, out_dtype, tq, native, nb):
    b, t, _ = zb3.shape
    mem_block = (nb,) + mk3.shape[1:]
    return pl.pallas_call(
        functools.partial(_mem_attn_kernel, native=native, nb=nb),
        grid=(b // nb, t // tq),
        in_specs=[
            pl.BlockSpec((nb, tq, D_MODEL), lambda i, j: (i, j, ZB_CQ)),
            pl.BlockSpec(mem_block, lambda i, j: (i, 0, 0)),
            pl.BlockSpec(mem_block, lambda i, j: (i, 0, 0)),
        ],
        out_specs=pl.BlockSpec((nb, tq, D_MODEL), lambda i, j: (i, j, 0)),
        out_shape=jax.ShapeDtypeStruct((b, t, D_MODEL), out_dtype),
        compiler_params=_params(("parallel", "arbitrary")),
        name="mem_attn",
    )(zb3, mk3, mv3)


def _merge_kernel(bl_ref, ba_ref, bm_ref, g0_ref, g1_ref, g2_ref, x_ref,
                  w0_ref, w1_ref, w2_ref, wo_ref, gpost_ref, gpre_ref, x1_ref, h2_ref):
    m = jax.nn.sigmoid(g0_ref[...]) * _dot(bl_ref[...].astype(BF16), w0_ref[...])
    m = m + jax.nn.sigmoid(g1_ref[...]) * _dot(ba_ref[...].astype(BF16), w1_ref[...])
    m = m + jax.nn.sigmoid(g2_ref[...]) * _dot(bm_ref[...].astype(BF16), w2_ref[...])
    y = _dot(m.astype(BF16), wo_ref[...])
    x1 = x_ref[...] + _rms(y) * gpost_ref[...]
    x1_ref[...] = x1
    h2_ref[...] = (_rms(x1) * gpre_ref[...]).astype(h2_ref.dtype)


def _merge(br_lru, br_attn, br_mem, zf, x, w0, w1, w2, wo, gpost, gpre, h2_dtype, tm):
    m = x.shape[0]
    tok = lambda i: (i, 0)
    cst = lambda i: (0, 0)
    wspec = pl.BlockSpec((D_MODEL, D_MODEL), cst)
    return pl.pallas_call(
        _merge_kernel,
        grid=(m // tm,),
        in_specs=[
            pl.BlockSpec((tm, D_MODEL), tok),
            pl.BlockSpec((tm, D_MODEL), tok),
            pl.BlockSpec((tm, D_MODEL), tok),
            pl.BlockSpec((tm, D_MODEL), lambda i: (i, 2)),
            pl.BlockSpec((tm, D_MODEL), lambda i: (i, 3)),
            pl.BlockSpec((tm, D_MODEL), lambda i: (i, 4)),
            pl.BlockSpec((tm, D_MODEL), tok),
            wspec, wspec, wspec, wspec,
            pl.BlockSpec((1, D_MODEL), cst),
            pl.BlockSpec((1, D_MODEL), cst),
        ],
        out_specs=[pl.BlockSpec((tm, D_MODEL), tok), pl.BlockSpec((tm, D_MODEL), tok)],
        out_shape=[jax.ShapeDtypeStruct((m, D_MODEL), F32),
                   jax.ShapeDtypeStruct((m, D_MODEL), h2_dtype)],
        compiler_params=_params(("parallel",)),
        name="merge_out_proj",
    )(br_lru, br_attn, br_mem, zf, zf, zf, x, w0, w1, w2, wo, gpost, gpre)


FFN_CHUNK = 512
FFN_AHEAD = 3


def _ffn_kernel(h_ref, x_ref, s0_ref, wup_ref, wc_ref, bc_ref, wdn_ref, g_ref,
                y_ref, cN_ref, up_ref, *, bb, tt):
    t = pl.program_id(1)
    W = FFN_CONV_W
    P = SUBLANE
    rows = bb * tt

    @pl.when(t == 0)
    def _():
        up_ref[:, P - (W - 1):P, :] = s0_ref[...]

    h = h_ref[...].reshape(rows, D_MODEL).astype(BF16)
    n_chunks = D_FF // FFN_CHUNK

    def chunk_cols(c):
        return [slice(base + c * FFN_CHUNK, base + (c + 1) * FFN_CHUNK) for base in (0, D_FF)]

    def up_project(c):
        for cs in chunk_cols(c):
            up_ref[:, P:, cs] = _dot(h, wup_ref[:, cs]).reshape(bb, tt, FFN_CHUNK)

    def activation(c):
        halves = []
        for cs in chunk_cols(c):
            uc = up_ref[:, P - 2:P - 2 + tt, cs] * wc_ref[0:1, cs]
            for j in range(1, W):
                uc = uc + up_ref[:, P - 2 + j:P - 2 + j + tt, cs] * wc_ref[j:j + 1, cs]
            halves.append((uc + bc_ref[:, cs]).reshape(rows, FFN_CHUNK))
        return (jax.nn.gelu(halves[0]) * halves[1]).astype(BF16)

    for c in range(min(FFN_AHEAD, n_chunks)):
        up_project(c)
    acc = jnp.zeros((rows, D_MODEL), F32)
    for c in range(n_chunks):
        if c + FFN_AHEAD < n_chunks:
            up_project(c + FFN_AHEAD)
        acc = acc + _dot(activation(c), wdn_ref[c * FFN_CHUNK:(c + 1) * FFN_CHUNK, :])

    tail = up_ref[:, tt + P - (W - 1):tt + P, :]
    up_ref[:, P - (W - 1):P, :] = tail
    cN_ref[...] = tail
    y = x_ref[...].reshape(rows, D_MODEL) + _rms(acc) * g_ref[...]
    y_ref[...] = y.reshape(bb, tt, D_MODEL)


def _ffn(h3, x3, s0, wup, wc, bc, wdn, g, bb, tt):
    b, t, _ = x3.shape
    kern = functools.partial(_ffn_kernel, bb=bb, tt=tt)
    cst = lambda i, j: (0, 0)
    once = pl.Buffered(1)
    return pl.pallas_call(
        kern,
        grid=(b // bb, t // tt),
        in_specs=[
            pl.BlockSpec((bb, tt, D_MODEL), lambda i, j: (i, j, 0)),
            pl.BlockSpec((bb, tt, D_MODEL), lambda i, j: (i, j, 0)),
            pl.BlockSpec((bb, FFN_CONV_W - 1, 2 * D_FF), lambda i, j: (i, 0, 0)),
            pl.BlockSpec((D_MODEL, 2 * D_FF), cst, pipeline_mode=once),
            pl.BlockSpec((FFN_CONV_W, 2 * D_FF), cst),
            pl.BlockSpec((1, 2 * D_FF), cst),
            pl.BlockSpec((D_FF, D_MODEL), cst, pipeline_mode=once),
            pl.BlockSpec((1, D_MODEL), cst),
        ],
        out_specs=[
            pl.BlockSpec((bb, tt, D_MODEL), lambda i, j: (i, j, 0)),
            pl.BlockSpec((bb, FFN_CONV_W - 1, 2 * D_FF), lambda i, j: (i, 0, 0)),
        ],
        out_shape=[
            jax.ShapeDtypeStruct((b, t, D_MODEL), F32),
            jax.ShapeDtypeStruct((b, FFN_CONV_W - 1, 2 * D_FF), F32),
        ],
        scratch_shapes=[pltpu.VMEM((bb, tt + SUBLANE, 2 * D_FF), F32)],
        compiler_params=_params(("parallel", "arbitrary")),
        name="conv_ffn",
    )(h3, x3, s0, wup, wc, bc, wdn, g)


def _layer(x3, attn_fn, mk3, mv3, lru_h0, lru_conv0, ffn_conv0, wts, *, prompt):
    b, t, _ = x3.shape
    m = b * t
    x2 = x3.reshape(m, D_MODEL)
    act_dtype = BF16 if prompt else F32
    tm = 512
    zf, zb, k, v = _in_proj(x2, wts["g_pre_mix"], wts["w_in"], wts["w_kt"], act_dtype,
                            IN_PROJ_TM, t if prompt else m, k_transposed=prompt)

    bb, tt = (1, 512) if prompt else (32, t)
    br_lru, lru_h, lru_conv = _lru(
        zf.reshape(b, t, -1), lru_conv0, lru_h0.reshape(b, 1, D_MODEL),
        wts["w_lru_conv"], wts["b_lru_conv"], wts["w_rg"], wts["b_rg_a"], wts["b_rg_i"],
        wts["lru_lambda"], act_dtype, bb, tt)

    br_attn = attn_fn(zb, k, v)
    br_mem = _mem_attn(zb.reshape(b, t, -1), mk3, mv3, act_dtype, 512 if prompt else t,
                        native=not prompt, nb=1 if prompt else 2)

    x1, h2 = _merge(br_lru.reshape(m, D_MODEL), br_attn.reshape(m, D_MODEL),
                    br_mem.reshape(m, D_MODEL), zf, x2,
                    wts["w_br_lru"], wts["w_br_attn"], wts["w_br_mem"], wts["w_out"],
                    wts["g_post_mix"], wts["g_pre_ffn"], act_dtype, tm)

    bb, tt = (1, 256) if prompt else (16, t)
    y, ffn_conv = _ffn(h2.reshape(b, t, D_MODEL), x1.reshape(b, t, D_MODEL), ffn_conv0,
                       wts["w_up"], wts["w_ffn_conv"], wts["b_ffn_conv"], wts["w_down"],
                       wts["g_post_ffn"], bb, tt)
    return y, k, v, lru_h.reshape(b, D_MODEL), lru_conv, ffn_conv


def _lambda_init(layer_idx):
    return 0.8 - 0.6 * math.exp(-0.3 * layer_idx)


def kernel(x_prompt, x_sample, mem_prompt, cache_k, cache_v, page_table, cache_mem_k, cache_mem_v, state_lru_h, state_lru_conv, state_ffn_conv, g_pre_mix, w_in, w_lru_conv, b_lru_conv, w_rg_a, b_rg_a, w_rg_i, b_rg_i, lru_lambda, lambda_q1, lambda_k1, lambda_q2, lambda_k2, g_subln, g_mem, w_mem_kv, w_br_lru, w_br_attn, w_br_mem, w_out, g_post_mix, g_pre_ffn, w_up, w_ffn_conv, b_ffn_conv, w_down, g_post_ffn):
    depth = w_in.shape[0]
    bp, tp, _ = x_prompt.shape
    bd, td, _ = x_sample.shape
    xp, xs = x_prompt, x_sample
    outs = [[] for _ in range(12)]
    row = lambda a: a.reshape(1, -1)

    for l in range(depth):
        lam_init = _lambda_init(l)
        wi = w_in[l]
        blk = lambda i: wi[:, i * D_MODEL:(i + 1) * D_MODEL]
        w_in_p = jnp.stack(
            [blk(0), blk(1), blk(6), blk(7), blk(8),
             blk(2) * (LOG2_E / math.sqrt(HEAD_DIM)), blk(3), blk(4),
             blk(5) * (1.0 / math.sqrt(MEM_HEAD_DIM))]).astype(BF16)
        wts = dict(
            g_pre_mix=row(g_pre_mix[l]), w_in=w_in_p, w_kt=blk(3).T.astype(BF16),
            w_lru_conv=w_lru_conv[l], b_lru_conv=row(b_lru_conv[l]),
            w_rg=jnp.concatenate([w_rg_a[l], w_rg_i[l]], axis=-1).astype(BF16),
            b_rg_a=row(b_rg_a[l]), b_rg_i=row(b_rg_i[l]), lru_lambda=row(lru_lambda[l]),
            w_br_lru=w_br_lru[l].astype(BF16), w_br_attn=w_br_attn[l].astype(BF16),
            w_br_mem=w_br_mem[l].astype(BF16), w_out=w_out[l].astype(BF16),
            g_post_mix=row(g_post_mix[l]), g_pre_ffn=row(g_pre_ffn[l]),
            w_up=w_up[l].astype(BF16), w_ffn_conv=w_ffn_conv[l], b_ffn_conv=row(b_ffn_conv[l]),
            w_down=w_down[l].astype(BF16), g_post_ffn=row(g_post_ffn[l]),
        )
        lams = (row(lambda_q1[l]), row(lambda_k1[l]), row(lambda_q2[l]), row(lambda_k2[l]))
        gs = row(g_subln[l])

        mkv = _mem_kv(mem_prompt.reshape(bp * N_MEM, D_MODEL), row(g_mem[l]),
                      w_mem_kv[l].astype(BF16))
        mk_p = mkv[0].reshape(bp, N_MEM, D_MODEL)
        mv_p = mkv[1].reshape(bp, N_MEM, D_MODEL)
        attn_p = lambda zb, k, v: _attn_prompt(zb, *lams, gs, bp, tp, ATTN_TQ, ATTN_HP, lam_init)
        xp, k_p, v_p, h_p, c_p, f_p = _layer(
            xp, attn_p, mk_p, mv_p,
            jnp.zeros((bp, D_MODEL), F32),
            jnp.zeros((bp, LRU_CONV_W - 1, D_MODEL), F32),
            jnp.zeros((bp, FFN_CONV_W - 1, 2 * D_FF), F32),
            wts, prompt=True)

        ck = cache_k[l].transpose(0, 2, 3, 4, 1).reshape(-1, D_MODEL, PAGE_SIZE)
        cv = cache_v[l].reshape(-1, PAGE_SIZE * N_HEADS, V_DIM)
        mem_view = lambda c: c.reshape(bd, N_MEM, MEM_HEADS, MEM_HEAD_DIM // LANE, LANE).transpose(
            0, 1, 3, 2, 4).reshape(bd, N_MEM * MEM_ROW_GROUP, LANE)
        attn_s = lambda zb, k, v: _attn_sample(
            page_table, zb.reshape(bd, td, -1), k.reshape(bd, td, D_MODEL),
            v.reshape(bd, td, D_MODEL), ck, cv, *lams, gs, lam_init)
        xs, k_s, v_s, h_s, c_s, f_s = _layer(
            xs, attn_s, mem_view(cache_mem_k[l]), mem_view(cache_mem_v[l]),
            state_lru_h[l], state_lru_conv[l], state_ffn_conv[l], wts, prompt=False)

        k_p = k_p.reshape(bp, N_HEADS, 2, HEAD_DIM, tp).transpose(0, 4, 1, 2, 3)
        vals = (k_p, v_p.reshape(bp, tp, N_HEADS, V_DIM),
                mk_p.reshape(bp, N_MEM, MEM_HEADS, MEM_HEAD_DIM),
                mv_p.reshape(bp, N_MEM, MEM_HEADS, MEM_HEAD_DIM), h_p, c_p, f_p,
                k_s.reshape(bd, td, N_HEADS, 2, HEAD_DIM), v_s.reshape(bd, td, N_HEADS, V_DIM),
                h_s, c_s, f_s)
        for o, val in zip(outs, vals):
            o.append(val)

    return (xp, xs, *[jnp.stack(o) for o in outs])
```

```python
import functools
import math

import jax
import jax.numpy as jnp
from jax import lax
from jax.experimental import pallas as pl
from jax.experimental.pallas import tpu as pltpu

F32 = jnp.float32
BF16 = jnp.bfloat16

D_MODEL = 1024
N_HEADS = 8
HEAD_DIM = 64
V_DIM = 128
N_MEM = 256
MEM_HEADS = 4
MEM_HEAD_DIM = 256
N_LRU_BLOCKS = 8
LRU_BLOCK = 128
LRU_CONV_W = 4
LRU_C = 8.0
D_FF = 3072
FFN_CONV_W = 3
RMS_EPS = 1e-6
NEG_INF = -1e30
LOG2_E = math.log2(math.e)
PAGE_SIZE = 128

N_F32_BLOCKS = 5
N_IN_BLOCKS = 9
ZB_Q, ZB_K, ZB_V, ZB_CQ = 0, 1, 2, 3

LANE = 128
SUBLANE = 8
VMEM_LIMIT = 56 * 1024 * 1024
IN_PROJ_TM = 256
ATTN_TQ = 512
ATTN_HP = 4
ATTN_CW = 256
ATTN_KH = 256
ATTN_AHEAD = 3


def _params(sem, vmem=VMEM_LIMIT):
    return pltpu.CompilerParams(dimension_semantics=sem, vmem_limit_bytes=vmem)


def _rms(x):
    return x * lax.rsqrt(jnp.mean(x * x, axis=-1, keepdims=True) + RMS_EPS)


def _dot(a, b):
    return jnp.dot(a, b, preferred_element_type=F32)


def _dot_nt(a, b):
    return lax.dot_general(a, b, (((1,), (1,)), ((), ())), preferred_element_type=F32)


def _dot_tn(a, b):
    return lax.dot_general(a, b, (((0,), (0,)), ((), ())), preferred_element_type=F32)


def _lam(q1, k1, q2, k2, lam_init):
    s1 = jnp.sum(q1 * k1, axis=-1, keepdims=True)
    s2 = jnp.sum(q2 * k2, axis=-1, keepdims=True)
    return jnp.exp(s1) - jnp.exp(s2) + lam_init


def _in_proj_kernel(x_ref, g_ref, w_ref, wkt_ref, zf_ref, zb_ref, k_ref, v_ref, h_ref, *,
                    k_transposed):
    h_ref[...] = (_rms(x_ref[...]) * g_ref[...]).astype(BF16)
    for n in range(N_F32_BLOCKS):
        zf_ref[:, n * D_MODEL:(n + 1) * D_MODEL] = _dot(h_ref[...], w_ref[n])
    for j in range(N_IN_BLOCKS - N_F32_BLOCKS):
        z = _dot(h_ref[...], w_ref[N_F32_BLOCKS + j])
        zb_ref[:, j * D_MODEL:(j + 1) * D_MODEL] = z.astype(zb_ref.dtype)
        if j == ZB_K and not k_transposed:
            k_ref[...] = z
        if j == ZB_V:
            v_ref[...] = z
    if k_transposed:
        k_ref[0] = _dot_nt(wkt_ref[...], h_ref[...])


def _in_proj(x, g, w_in, w_kt, zb_dtype, tm, seq_len, k_transposed):
    m = x.shape[0]
    nb = N_F32_BLOCKS
    nt = seq_len // tm
    if k_transposed:
        k_spec = pl.BlockSpec((1, D_MODEL, tm), lambda i: (i // nt, 0, i % nt))
        k_shape = jax.ShapeDtypeStruct((m // seq_len, D_MODEL, seq_len), F32)
    else:
        k_spec = pl.BlockSpec((tm, D_MODEL), lambda i: (i, 0))
        k_shape = jax.ShapeDtypeStruct((m, D_MODEL), F32)
    tok = lambda i: (i, 0)
    return pl.pallas_call(
        functools.partial(_in_proj_kernel, k_transposed=k_transposed),
        grid=(m // tm,),
        in_specs=[
            pl.BlockSpec((tm, D_MODEL), tok),
            pl.BlockSpec((1, D_MODEL), lambda i: (0, 0)),
            pl.BlockSpec((N_IN_BLOCKS, D_MODEL, D_MODEL), lambda i: (0, 0, 0),
                         pipeline_mode=pl.Buffered(1)),
            pl.BlockSpec((D_MODEL, D_MODEL), lambda i: (0, 0), pipeline_mode=pl.Buffered(1)),
        ],
        out_specs=[
            pl.BlockSpec((tm, nb * D_MODEL), tok),
            pl.BlockSpec((tm, (N_IN_BLOCKS - nb) * D_MODEL), tok),
            k_spec,
            pl.BlockSpec((tm, D_MODEL), tok),
        ],
        out_shape=[
            jax.ShapeDtypeStruct((m, nb * D_MODEL), F32),
            jax.ShapeDtypeStruct((m, (N_IN_BLOCKS - nb) * D_MODEL), zb_dtype),
            k_shape,
            jax.ShapeDtypeStruct((m, D_MODEL), F32),
        ],
        scratch_shapes=[pltpu.VMEM((tm, D_MODEL), BF16)],
        compiler_params=_params(("parallel",)),
        name="in_proj",
    )(x, g, w_in, w_kt)


def _lru_kernel(lx_ref, ly_ref, s0_ref, h0_ref, wc_ref, bc_ref, wrg_ref, ba_ref, bi_ref,
                lam_ref, out_ref, hN_ref, cN_ref, xp_ref, a_ref, u_ref, hc_ref, *, bb, tt):
    t = pl.program_id(1)
    W = LRU_CONV_W
    P = SUBLANE

    @pl.when(t == 0)
    def _():
        xp_ref[:, P - (W - 1):P, :] = s0_ref[...]
        hc_ref[...] = h0_ref[...]

    xp_ref[:, P:, :] = lx_ref[...]
    xc = xp_ref[:, P - 3:P - 3 + tt, :] * wc_ref[0:1, :]
    for j in range(1, W):
        xc = xc + xp_ref[:, P - 3 + j:P - 3 + j + tt, :] * wc_ref[j:j + 1, :]
    xc = xc + bc_ref[...]
    tail = xp_ref[:, tt + P - (W - 1):tt + P, :]
    xp_ref[:, P - (W - 1):P, :] = tail
    cN_ref[...] = tail

    x2 = xc.reshape(bb * tt, D_MODEL)
    xb = x2.astype(BF16)
    lin = [_dot(xb[:, n * LRU_BLOCK:(n + 1) * LRU_BLOCK], wrg_ref[n])
           for n in range(N_LRU_BLOCKS)]
    r_lin = jnp.concatenate([z[:, :LRU_BLOCK] for z in lin], axis=-1)
    i_lin = jnp.concatenate([z[:, LRU_BLOCK:] for z in lin], axis=-1)
    r = jax.nn.sigmoid(r_lin + ba_ref[...])
    gi = jax.nn.sigmoid(i_lin + bi_ref[...])
    neg_lam = -lam_ref[...]
    softplus = jnp.maximum(neg_lam, 0.0) + jnp.log1p(jnp.exp(-jnp.abs(neg_lam)))
    log_a = (-LRU_C * softplus) * r
    a = jnp.exp(log_a)
    u = jnp.sqrt(1.0 - a * a) * (gi * x2)

    g8 = (bb * tt) // SUBLANE
    a3 = a.reshape(g8, SUBLANE, D_MODEL)
    u3 = u.reshape(g8, SUBLANE, D_MODEL)
    row = lax.broadcasted_iota(jnp.int32, (g8, SUBLANE, D_MODEL), 1)
    for s in (1, 2, 4):
        a_sh = pltpu.roll(a3, s, 1)
        u_sh = pltpu.roll(u3, s, 1)
        ok = row >= s
        u3 = jnp.where(ok, a3 * u_sh + u3, u3)
        a3 = jnp.where(ok, a3 * a_sh, a3)
    a_ref[...] = a3.reshape(bb, tt, D_MODEL)
    u_ref[...] = u3.reshape(bb, tt, D_MODEL)

    def body(g, h):
        off = pl.multiple_of(g * SUBLANE, SUBLANE)
        hg = a_ref[:, pl.ds(off, SUBLANE), :] * h + u_ref[:, pl.ds(off, SUBLANE), :]
        u_ref[:, pl.ds(off, SUBLANE), :] = hg
        return hg[:, SUBLANE - 1:SUBLANE, :]

    h_last = lax.fori_loop(0, tt // SUBLANE, body, hc_ref[...])
    hc_ref[...] = h_last
    hN_ref[...] = h_last
    out_ref[...] = (jax.nn.gelu(ly_ref[...]) * u_ref[...]).astype(out_ref.dtype)


def _lru(zf3, s0, h0, wc, bc, wrg, ba, bi, lam, out_dtype, bb, tt):
    b, t, _ = zf3.shape
    kern = functools.partial(_lru_kernel, bb=bb, tt=tt)
    const2 = lambda i, j: (0, 0)
    return pl.pallas_call(
        kern,
        grid=(b // bb, t // tt),
        in_specs=[
            pl.BlockSpec((bb, tt, D_MODEL), lambda i, j: (i, j, 0)),
            pl.BlockSpec((bb, tt, D_MODEL), lambda i, j: (i, j, 1)),
            pl.BlockSpec((bb, LRU_CONV_W - 1, D_MODEL), lambda i, j: (i, 0, 0)),
            pl.BlockSpec((bb, 1, D_MODEL), lambda i, j: (i, 0, 0)),
            pl.BlockSpec((LRU_CONV_W, D_MODEL), const2),
            pl.BlockSpec((1, D_MODEL), const2),
            pl.BlockSpec((N_LRU_BLOCKS, LRU_BLOCK, 2 * LRU_BLOCK), lambda i, j: (0, 0, 0)),
            pl.BlockSpec((1, D_MODEL), const2),
            pl.BlockSpec((1, D_MODEL), const2),
            pl.BlockSpec((1, D_MODEL), const2),
        ],
        out_specs=[
            pl.BlockSpec((bb, tt, D_MODEL), lambda i, j: (i, j, 0)),
            pl.BlockSpec((bb, 1, D_MODEL), lambda i, j: (i, 0, 0)),
            pl.BlockSpec((bb, LRU_CONV_W - 1, D_MODEL), lambda i, j: (i, 0, 0)),
        ],
        out_shape=[
            jax.ShapeDtypeStruct((b, t, D_MODEL), out_dtype),
            jax.ShapeDtypeStruct((b, 1, D_MODEL), F32),
            jax.ShapeDtypeStruct((b, LRU_CONV_W - 1, D_MODEL), F32),
        ],
        scratch_shapes=[
            pltpu.VMEM((bb, tt + SUBLANE, D_MODEL), F32),
            pltpu.VMEM((bb, tt, D_MODEL), F32),
            pltpu.VMEM((bb, tt, D_MODEL), F32),
            pltpu.VMEM((bb, 1, D_MODEL), F32),
        ],
        compiler_params=_params(("parallel", "arbitrary")),
        name="rg_lru",
    )(zf3, zf3, s0, h0, wc, bc, wrg, ba, bi, lam)


def _attn_prompt_kernel(q_ref, k_ref, v_ref, lq1_ref, lk1_ref, lq2_ref, lk2_ref, gs_ref,
                        o_ref, qq_ref, m_ref, l_ref, acc_ref, *, tq, hp, lam_init):
    qi = pl.program_id(2)
    for hh in range(hp):
        q = q_ref[:, hh * V_DIM:(hh + 1) * V_DIM]
        lane = lax.broadcasted_iota(jnp.int32, q.shape, 1)
        zero = jnp.zeros_like(q)
        qq_ref[hh, :tq, :] = jnp.where(lane < HEAD_DIM, q, zero)
        qq_ref[hh, tq:, :] = jnp.where(lane >= HEAD_DIM, q, zero)
    m_ref[...] = jnp.full(m_ref.shape, NEG_INF, F32)
    l_ref[...] = jnp.zeros(l_ref.shape, F32)
    acc_ref[...] = jnp.zeros(acc_ref.shape, F32)

    def run(tiles):
        offs = [pl.multiple_of(ki * tq, tq) for ki, _ in tiles]
        chunks = [(ti, hh, c0) for ti in range(len(tiles)) for hh in range(hp)
                  for c0 in range(0, 2 * tq, ATTN_CW)]

        def n_keys(ti, c0):
            return min(tq, c0 % tq + ATTN_CW) if tiles[ti][1] else tq

        def scores(ti, hh, c0):
            qc = qq_ref[hh, c0:c0 + ATTN_CW, :]
            s = jnp.concatenate(
                [_dot_nt(k_ref[pl.ds(offs[ti] + r0, ATTN_KH), hh * V_DIM:(hh + 1) * V_DIM], qc)
                 for r0 in range(0, n_keys(ti, c0), ATTN_KH)], axis=0)
            if tiles[ti][1]:
                key = lax.broadcasted_iota(jnp.int32, s.shape, 0)
                qry = lax.broadcasted_iota(jnp.int32, s.shape, 1) + c0 % tq
                s = jnp.where(key <= qry, s, NEG_INF)
            return s

        def softmax(ti, hh, c0, s):
            cols = slice(c0, c0 + ATTN_CW)
            m_old = m_ref[hh, :, cols]
            m_new = jnp.maximum(m_old, jnp.max(s, axis=0, keepdims=True))
            alpha = jnp.exp2(m_old - m_new)
            p = jnp.exp2(s - m_new)
            l_ref[hh, :, cols] = alpha * l_ref[hh, :, cols] + jnp.sum(p, axis=0, keepdims=True)
            m_ref[hh, :, cols] = m_new
            return alpha, p.astype(BF16)

        def values(ti, hh, c0, alpha, p):
            cols = slice(c0, c0 + ATTN_CW)
            pv = None
            for r0 in range(0, n_keys(ti, c0), ATTN_KH):
                vt = v_ref[pl.ds(offs[ti] + r0, ATTN_KH), hh * V_DIM:(hh + 1) * V_DIM]
                part = _dot_tn(vt, p[r0:r0 + ATTN_KH, :])
                pv = part if pv is None else pv + part
            acc_ref[hh, :, cols] = alpha * acc_ref[hh, :, cols] + pv

        n = len(chunks)
        s_q = {i: scores(*chunks[i]) for i in range(min(ATTN_AHEAD, n))}
        p_q = {0: softmax(*chunks[0], s_q.pop(0))}
        for i in range(n):
            if i + ATTN_AHEAD < n:
                s_q[i + ATTN_AHEAD] = scores(*chunks[i + ATTN_AHEAD])
            if i + 1 < n:
                p_q[i + 1] = softmax(*chunks[i + 1], s_q.pop(i + 1))
            values(*chunks[i], *p_q.pop(i))

    def body(j, c):
        run([(2 * j, False), (2 * j + 1, False)])
        return c

    lax.fori_loop(0, qi // 2, body, 0)

    @pl.when(qi % 2 == 1)
    def _():
        run([(qi - 1, False), (qi, True)])

    @pl.when(qi % 2 == 0)
    def _():
        run([(qi, True)])


    lam = _lam(lq1_ref[...], lk1_ref[...], lq2_ref[...], lk2_ref[...], lam_init)
    for hh in range(hp):
        o_t = acc_ref[hh] * (1.0 / l_ref[hh])
        o_t = o_t[:, :tq] - lam * o_t[:, tq:]
        o = o_t.T
        o_ref[:, hh * V_DIM:(hh + 1) * V_DIM] = (
            (_rms(o) * gs_ref[...]) * (1.0 - lam_init)).astype(o_ref.dtype)


def _attn_prompt(zb, lq1, lk1, lq2, lk2, gs, b, t, tq, hp, lam_init):
    m = zb.shape[0]
    nq = t // tq
    ng = N_HEADS // hp
    w = hp * V_DIM
    kern = functools.partial(_attn_prompt_kernel, tq=tq, hp=hp, lam_init=lam_init)
    vec = lambda i, h, j: (0, 0)
    return pl.pallas_call(
        kern,
        grid=(b, ng, nq),
        in_specs=[
            pl.BlockSpec((tq, w), lambda i, h, j: (i * nq + j, ZB_Q * ng + h)),
            pl.BlockSpec((t, w), lambda i, h, j: (i, ZB_K * ng + h), pipeline_mode=pl.Buffered(1)),
            pl.BlockSpec((t, w), lambda i, h, j: (i, ZB_V * ng + h), pipeline_mode=pl.Buffered(1)),
            pl.BlockSpec((1, HEAD_DIM), vec),
            pl.BlockSpec((1, HEAD_DIM), vec),
            pl.BlockSpec((1, HEAD_DIM), vec),
            pl.BlockSpec((1, HEAD_DIM), vec),
            pl.BlockSpec((1, V_DIM), vec),
        ],
        out_specs=pl.BlockSpec((tq, w), lambda i, h, j: (i * nq + j, h)),
        out_shape=jax.ShapeDtypeStruct((m, N_HEADS * V_DIM), BF16),
        scratch_shapes=[
            pltpu.VMEM((hp, 2 * tq, V_DIM), BF16),
            pltpu.VMEM((hp, 1, 2 * tq), F32),
            pltpu.VMEM((hp, 1, 2 * tq), F32),
            pltpu.VMEM((hp, V_DIM, 2 * tq), F32),
        ],
        compiler_params=_params(("parallel", "parallel", "arbitrary")),
        name="diff_attn_prompt",
    )(zb, zb, zb, lq1, lk1, lq2, lk2, gs)


def _attn_sample_kernel(pt_ref, q_ref, kn_ref, vn_ref, *rest, n_pages, tn, lam_init):
    del pt_ref
    kp_refs = rest[:n_pages]
    vp_refs = rest[n_pages:2 * n_pages]
    lq1_ref, lk1_ref, lq2_ref, lk2_ref, gs_ref, o_ref, s_ref = rest[2 * n_pages:]
    nrow = 2 * N_HEADS * tn
    n_past = n_pages * PAGE_SIZE
    HPB = 2
    nblk = N_HEADS // HPB
    rb = nrow // nblk
    fb = HPB * V_DIM

    q = q_ref[0]
    qt = jnp.concatenate([q] * (nrow // tn), axis=0)
    rj = lax.broadcasted_iota(jnp.int32, qt.shape, 0)
    cf = lax.broadcasted_iota(jnp.int32, qt.shape, 1)
    pair_j = rj >> (tn.bit_length() - 1)
    feat_blk = cf >> (HEAD_DIM.bit_length() - 1)
    qbd = jnp.where(feat_blk == pair_j, qt, 0.0)
    qblk = [qbd[g * rb:(g + 1) * rb, g * fb:(g + 1) * fb] for g in range(nblk)]

    for p in range(n_pages):
        for g in range(nblk):
            s_ref[g * rb:(g + 1) * rb, p * PAGE_SIZE:(p + 1) * PAGE_SIZE] = _dot(
                qblk[g], kp_refs[p][0, g * fb:(g + 1) * fb, :])
    s_new = _dot_nt(qbd, kn_ref[0])
    qry = lax.broadcasted_iota(jnp.int32, s_new.shape, 0) & (tn - 1)
    key = lax.broadcasted_iota(jnp.int32, s_new.shape, 1)
    s_new = jnp.where(key <= qry, s_new, NEG_INF)

    s = s_ref[...]
    mx = jnp.maximum(jnp.max(s, axis=-1, keepdims=True), jnp.max(s_new, axis=-1, keepdims=True))
    e = jnp.exp2(s - mx)
    e_new = jnp.exp2(s_new - mx)
    inv_l = 1.0 / (jnp.sum(e, axis=-1, keepdims=True) + jnp.sum(e_new, axis=-1, keepdims=True))
    s_ref[...] = e * inv_l
    p_new = e_new * inv_l

    lam = _lam(lq1_ref[...], lk1_ref[...], lq2_ref[...], lk2_ref[...], lam_init)
    outs = []
    for g in range(nblk):
        rows = slice(g * rb, (g + 1) * rb)
        acc = _dot(p_new[rows, :], vn_ref[0, :, g * fb:(g + 1) * fb])
        for p in range(n_pages):
            vg = jnp.concatenate(
                [vp_refs[p][0, pl.ds(g * HPB + hl, PAGE_SIZE, stride=N_HEADS), :]
                 for hl in range(HPB)], axis=-1)
            acc = acc + _dot(s_ref[rows, p * PAGE_SIZE:(p + 1) * PAGE_SIZE], vg)
        for hl in range(HPB):
            r0 = hl * 2 * tn
            cs = slice(hl * V_DIM, (hl + 1) * V_DIM)
            o = acc[r0:r0 + tn, cs] - lam * acc[r0 + tn:r0 + 2 * tn, cs]
            outs.append((_rms(o) * gs_ref[...]) * (1.0 - lam_init))
    o_ref[0] = jnp.concatenate(outs, axis=-1).astype(o_ref.dtype)


def _attn_sample(page_table, zb3, k3, v3, cache_kt, cache_v2, lq1, lk1, lq2, lk2, gs, lam_init):
    bd, tn, _ = k3.shape
    n_pages = page_table.shape[1]
    kern = functools.partial(_attn_sample_kernel, n_pages=n_pages, tn=tn, lam_init=lam_init)
    vec = lambda i, pt: (0, 0)

    def page_spec(p):
        return pl.BlockSpec((1, D_MODEL, PAGE_SIZE), lambda i, pt: (pt[i, p], 0, 0))

    grid_spec = pltpu.PrefetchScalarGridSpec(
        num_scalar_prefetch=1,
        grid=(bd,),
        in_specs=[
            pl.BlockSpec((1, tn, D_MODEL), lambda i, pt: (i, 0, ZB_Q)),
            pl.BlockSpec((1, tn, D_MODEL), lambda i, pt: (i, 0, 0)),
            pl.BlockSpec((1, tn, D_MODEL), lambda i, pt: (i, 0, 0)),
            *[page_spec(p) for p in range(n_pages)],
            *[page_spec(p) for p in range(n_pages)],
            pl.BlockSpec((1, HEAD_DIM), vec),
            pl.BlockSpec((1, HEAD_DIM), vec),
            pl.BlockSpec((1, HEAD_DIM), vec),
            pl.BlockSpec((1, HEAD_DIM), vec),
            pl.BlockSpec((1, V_DIM), vec),
        ],
        out_specs=pl.BlockSpec((1, tn, D_MODEL), lambda i, pt: (i, 0, 0)),
        scratch_shapes=[pltpu.VMEM((2 * N_HEADS * tn, n_pages * PAGE_SIZE), F32)],
    )
    return pl.pallas_call(
        kern,
        grid_spec=grid_spec,
        out_shape=jax.ShapeDtypeStruct((bd, tn, D_MODEL), F32),
        compiler_params=_params(("arbitrary",)),
        name="diff_attn_sample",
    )(page_table, zb3, k3, v3, *([cache_kt] * n_pages), *([cache_v2] * n_pages),
      lq1, lk1, lq2, lk2, gs)


def _mem_kv_kernel(x_ref, g_ref, w_ref, o_ref):
    h = (_rms(x_ref[...]) * g_ref[...]).astype(BF16)
    o_ref[0] = _dot(h, w_ref[...])


def _mem_kv(mem2, g, w):
    m = mem2.shape[0]
    n = w.shape[1]
    return pl.pallas_call(
        _mem_kv_kernel,
        grid=(n // D_MODEL,),
        in_specs=[
            pl.BlockSpec((m, D_MODEL), lambda j: (0, 0)),
            pl.BlockSpec((1, D_MODEL), lambda j: (0, 0)),
            pl.BlockSpec((D_MODEL, D_MODEL), lambda j: (0, j)),
        ],
        out_specs=pl.BlockSpec((1, m, D_MODEL), lambda j: (j, 0, 0)),
        out_shape=jax.ShapeDtypeStruct((n // D_MODEL, m, D_MODEL), F32),
        compiler_params=_params(("parallel",)),
        name="mem_kv",
    )(mem2, g, w)


MEM_ROW_GROUP = MEM_HEADS * (MEM_HEAD_DIM // LANE)


def _mem_attn_kernel(q_ref, mk_ref, mv_ref, o_ref, *, native, nb):
    mxu_dtype = q_ref.dtype

    def head(ref, i, h):
        if native:
            return jnp.concatenate(
                [ref[i, pl.ds(dc * MEM_HEADS + h, N_MEM, stride=MEM_ROW_GROUP), :]
                 for dc in range(MEM_HEAD_DIM // LANE)], axis=-1)
        return ref[i, :, h * MEM_HEAD_DIM:(h + 1) * MEM_HEAD_DIM].astype(mxu_dtype)

    pairs = [(i, h) for i in range(nb) for h in range(MEM_HEADS)]
    scores = [_dot_nt(q_ref[i, :, h * MEM_HEAD_DIM:(h + 1) * MEM_HEAD_DIM], head(mk_ref, i, h))
              for i, h in pairs]
    probs = []
    for s in scores:
        e = jnp.exp(s - jnp.max(s, axis=-1, keepdims=True))
        probs.append((e / jnp.sum(e, axis=-1, keepdims=True)).astype(mxu_dtype))
    outs = [_dot(p, head(mv_ref, i, h)) for p, (i, h) in zip(probs, pairs)]
    for i in range(nb):
        o_ref[i] = jnp.concatenate(outs[i * MEM_HEADS:(i + 1) * MEM_HEADS],
                                   axis=-1).astype(o_ref.dtype)


def _mem_attn(zb3, mk3, mv3, out_dtype, tq, native, nb):
    b, t, _ = zb3.shape
    mem_block = (nb,) + mk3.shape[1:]
    return pl.pallas_call(
        functools.partial(_mem_attn_kernel, native=native, nb=nb),
        grid=(b // nb, t // tq),
        in_specs=[
            pl.BlockSpec((nb, tq, D_MODEL), lambda i, j: (i, j, ZB_CQ)),
            pl.BlockSpec(mem_block, lambda i, j: (i, 0, 0)),
            pl.BlockSpec(mem_block, lambda i, j: (i, 0, 0)),
        ],
        out_specs=pl.BlockSpec((nb, tq, D_MODEL), lambda i, j: (i, j, 0)),
        out_shape=jax.ShapeDtypeStruct((b, t, D_MODEL), out_dtype),
        compiler_params=_params(("parallel", "arbitrary")),
        name="mem_attn",
    )(zb3, mk3, mv3)


def _merge_kernel(bl_ref, ba_ref, bm_ref, g0_ref, g1_ref, g2_ref, x_ref,
                  w0_ref, w1_ref, w2_ref, wo_ref, gpost_ref, gpre_ref, x1_ref, h2_ref):
    m = jax.nn.sigmoid(g0_ref[...]) * _dot(bl_ref[...].astype(BF16), w0_ref[...])
    m = m + jax.nn.sigmoid(g1_ref[...]) * _dot(ba_ref[...].astype(BF16), w1_ref[...])
    m = m + jax.nn.sigmoid(g2_ref[...]) * _dot(bm_ref[...].astype(BF16), w2_ref[...])
    y = _dot(m.astype(BF16), wo_ref[...])
    x1 = x_ref[...] + _rms(y) * gpost_ref[...]
    x1_ref[...] = x1
    h2_ref[...] = (_rms(x1) * gpre_ref[...]).astype(h2_ref.dtype)


def _merge(br_lru, br_attn, br_mem, zf, x, w0, w1, w2, wo, gpost, gpre, h2_dtype, tm):
    m = x.shape[0]
    tok = lambda i: (i, 0)
    cst = lambda i: (0, 0)
    wspec = pl.BlockSpec((D_MODEL, D_MODEL), cst)
    return pl.pallas_call(
        _merge_kernel,
        grid=(m // tm,),
        in_specs=[
            pl.BlockSpec((tm, D_MODEL), tok),
            pl.BlockSpec((tm, D_MODEL), tok),
            pl.BlockSpec((tm, D_MODEL), tok),
            pl.BlockSpec((tm, D_MODEL), lambda i: (i, 2)),
            pl.BlockSpec((tm, D_MODEL), lambda i: (i, 3)),
            pl.BlockSpec((tm, D_MODEL), lambda i: (i, 4)),
            pl.BlockSpec((tm, D_MODEL), tok),
            wspec, wspec, wspec, wspec,
            pl.BlockSpec((1, D_MODEL), cst),
            pl.BlockSpec((1, D_MODEL), cst),
        ],
        out_specs=[pl.BlockSpec((tm, D_MODEL), tok), pl.BlockSpec((tm, D_MODEL), tok)],
        out_shape=[jax.ShapeDtypeStruct((m, D_MODEL), F32),
                   jax.ShapeDtypeStruct((m, D_MODEL), h2_dtype)],
        compiler_params=_params(("parallel",)),
        name="merge_out_proj",
    )(br_lru, br_attn, br_mem, zf, zf, zf, x, w0, w1, w2, wo, gpost, gpre)


FFN_CHUNK = 1024
FFN_AHEAD = 2


def _ffn_kernel(h_ref, x_ref, s0_ref, wup_ref, wc_ref, bc_ref, wdn_ref, g_ref,
                y_ref, cN_ref, up_ref, *, bb, tt):
    t = pl.program_id(1)
    W = FFN_CONV_W
    P = SUBLANE
    rows = bb * tt

    @pl.when(t == 0)
    def _():
        up_ref[:, P - (W - 1):P, :] = s0_ref[...]

    h = h_ref[...].reshape(rows, D_MODEL).astype(BF16)
    n_chunks = D_FF // FFN_CHUNK

    def chunk_cols(c):
        return [slice(base + c * FFN_CHUNK, base + (c + 1) * FFN_CHUNK) for base in (0, D_FF)]

    def up_project(c):
        for cs in chunk_cols(c):
            up_ref[:, P:, cs] = _dot(h, wup_ref[:, cs]).reshape(bb, tt, FFN_CHUNK)

    def activation(c):
        halves = []
        for cs in chunk_cols(c):
            uc = up_ref[:, P - 2:P - 2 + tt, cs] * wc_ref[0:1, cs]
            for j in range(1, W):
                uc = uc + up_ref[:, P - 2 + j:P - 2 + j + tt, cs] * wc_ref[j:j + 1, cs]
            halves.append((uc + bc_ref[:, cs]).reshape(rows, FFN_CHUNK))
        return (jax.nn.gelu(halves[0]) * halves[1]).astype(BF16)

    for c in range(min(FFN_AHEAD, n_chunks)):
        up_project(c)
    acc = jnp.zeros((rows, D_MODEL), F32)
    for c in range(n_chunks):
        if c + FFN_AHEAD < n_chunks:
            up_project(c + FFN_AHEAD)
        acc = acc + _dot(activation(c), wdn_ref[c * FFN_CHUNK:(c + 1) * FFN_CHUNK, :])

    tail = up_ref[:, tt + P - (W - 1):tt + P, :]
    up_ref[:, P - (W - 1):P, :] = tail
    cN_ref[...] = tail
    y = x_ref[...].reshape(rows, D_MODEL) + _rms(acc) * g_ref[...]
    y_ref[...] = y.reshape(bb, tt, D_MODEL)


def _ffn(h3, x3, s0, wup, wc, bc, wdn, g, bb, tt):
    b, t, _ = x3.shape
    kern = functools.partial(_ffn_kernel, bb=bb, tt=tt)
    cst = lambda i, j: (0, 0)
    once = pl.Buffered(1)
    return pl.pallas_call(
        kern,
        grid=(b // bb, t // tt),
        in_specs=[
            pl.BlockSpec((bb, tt, D_MODEL), lambda i, j: (i, j, 0)),
            pl.BlockSpec((bb, tt, D_MODEL), lambda i, j: (i, j, 0)),
            pl.BlockSpec((bb, FFN_CONV_W - 1, 2 * D_FF), lambda i, j: (i, 0, 0)),
            pl.BlockSpec((D_MODEL, 2 * D_FF), cst, pipeline_mode=once),
            pl.BlockSpec((FFN_CONV_W, 2 * D_FF), cst),
            pl.BlockSpec((1, 2 * D_FF), cst),
            pl.BlockSpec((D_FF, D_MODEL), cst, pipeline_mode=once),
            pl.BlockSpec((1, D_MODEL), cst),
        ],
        out_specs=[
            pl.BlockSpec((bb, tt, D_MODEL), lambda i, j: (i, j, 0)),
            pl.BlockSpec((bb, FFN_CONV_W - 1, 2 * D_FF), lambda i, j: (i, 0, 0)),
        ],
        out_shape=[
            jax.ShapeDtypeStruct((b, t, D_MODEL), F32),
            jax.ShapeDtypeStruct((b, FFN_CONV_W - 1, 2 * D_FF), F32),
        ],
        scratch_shapes=[pltpu.VMEM((bb, tt + SUBLANE, 2 * D_FF), F32)],
        compiler_params=_params(("parallel", "arbitrary")),
        name="conv_ffn",
    )(h3, x3, s0, wup, wc, bc, wdn, g)


def _layer(x3, attn_fn, mk3, mv3, lru_h0, lru_conv0, ffn_conv0, wts, *, prompt):
    b, t, _ = x3.shape
    m = b * t
    x2 = x3.reshape(m, D_MODEL)
    act_dtype = BF16 if prompt else F32
    tm = 512
    zf, zb, k, v = _in_proj(x2, wts["g_pre_mix"], wts["w_in"], wts["w_kt"], act_dtype,
                            IN_PROJ_TM, t if prompt else m, k_transposed=prompt)

    bb, tt = (1, 512) if prompt else (32, t)
    br_lru, lru_h, lru_conv = _lru(
        zf.reshape(b, t, -1), lru_conv0, lru_h0.reshape(b, 1, D_MODEL),
        wts["w_lru_conv"], wts["b_lru_conv"], wts["w_rg"], wts["b_rg_a"], wts["b_rg_i"],
        wts["lru_lambda"], act_dtype, bb, tt)

    br_attn = attn_fn(zb, k, v)
    br_mem = _mem_attn(zb.reshape(b, t, -1), mk3, mv3, act_dtype, 512 if prompt else t,
                        native=not prompt, nb=1 if prompt else 4)

    x1, h2 = _merge(br_lru.reshape(m, D_MODEL), br_attn.reshape(m, D_MODEL),
                    br_mem.reshape(m, D_MODEL), zf, x2,
                    wts["w_br_lru"], wts["w_br_attn"], wts["w_br_mem"], wts["w_out"],
                    wts["g_post_mix"], wts["g_pre_ffn"], act_dtype, tm)

    bb, tt = (1, 256) if prompt else (16, t)
    y, ffn_conv = _ffn(h2.reshape(b, t, D_MODEL), x1.reshape(b, t, D_MODEL), ffn_conv0,
                       wts["w_up"], wts["w_ffn_conv"], wts["b_ffn_conv"], wts["w_down"],
                       wts["g_post_ffn"], bb, tt)
    return y, k, v, lru_h.reshape(b, D_MODEL), lru_conv, ffn_conv


def _lambda_init(layer_idx):
    return 0.8 - 0.6 * math.exp(-0.3 * layer_idx)


def kernel(x_prompt, x_sample, mem_prompt, cache_k, cache_v, page_table, cache_mem_k, cache_mem_v, state_lru_h, state_lru_conv, state_ffn_conv, g_pre_mix, w_in, w_lru_conv, b_lru_conv, w_rg_a, b_rg_a, w_rg_i, b_rg_i, lru_lambda, lambda_q1, lambda_k1, lambda_q2, lambda_k2, g_subln, g_mem, w_mem_kv, w_br_lru, w_br_attn, w_br_mem, w_out, g_post_mix, g_pre_ffn, w_up, w_ffn_conv, b_ffn_conv, w_down, g_post_ffn):
    depth = w_in.shape[0]
    bp, tp, _ = x_prompt.shape
    bd, td, _ = x_sample.shape
    xp, xs = x_prompt, x_sample
    outs = [[] for _ in range(12)]
    row = lambda a: a.reshape(1, -1)

    for l in range(depth):
        lam_init = _lambda_init(l)
        wi = w_in[l]
        blk = lambda i: wi[:, i * D_MODEL:(i + 1) * D_MODEL]
        w_in_p = jnp.stack(
            [blk(0), blk(1), blk(6), blk(7), blk(8),
             blk(2) * (LOG2_E / math.sqrt(HEAD_DIM)), blk(3), blk(4),
             blk(5) * (1.0 / math.sqrt(MEM_HEAD_DIM))]).astype(BF16)
        wts = dict(
            g_pre_mix=row(g_pre_mix[l]), w_in=w_in_p, w_kt=blk(3).T.astype(BF16),
            w_lru_conv=w_lru_conv[l], b_lru_conv=row(b_lru_conv[l]),
            w_rg=jnp.concatenate([w_rg_a[l], w_rg_i[l]], axis=-1).astype(BF16),
            b_rg_a=row(b_rg_a[l]), b_rg_i=row(b_rg_i[l]), lru_lambda=row(lru_lambda[l]),
            w_br_lru=w_br_lru[l].astype(BF16), w_br_attn=w_br_attn[l].astype(BF16),
            w_br_mem=w_br_mem[l].astype(BF16), w_out=w_out[l].astype(BF16),
            g_post_mix=row(g_post_mix[l]), g_pre_ffn=row(g_pre_ffn[l]),
            w_up=w_up[l].astype(BF16), w_ffn_conv=w_ffn_conv[l], b_ffn_conv=row(b_ffn_conv[l]),
            w_down=w_down[l].astype(BF16), g_post_ffn=row(g_post_ffn[l]),
        )
        lams = (row(lambda_q1[l]), row(lambda_k1[l]), row(lambda_q2[l]), row(lambda_k2[l]))
        gs = row(g_subln[l])

        mkv = _mem_kv(mem_prompt.reshape(bp * N_MEM, D_MODEL), row(g_mem[l]),
                      w_mem_kv[l].astype(BF16))
        mk_p = mkv[0].reshape(bp, N_MEM, D_MODEL)
        mv_p = mkv[1].reshape(bp, N_MEM, D_MODEL)
        attn_p = lambda zb, k, v: _attn_prompt(zb, *lams, gs, bp, tp, ATTN_TQ, ATTN_HP, lam_init)
        xp, k_p, v_p, h_p, c_p, f_p = _layer(
            xp, attn_p, mk_p, mv_p,
            jnp.zeros((bp, D_MODEL), F32),
            jnp.zeros((bp, LRU_CONV_W - 1, D_MODEL), F32),
            jnp.zeros((bp, FFN_CONV_W - 1, 2 * D_FF), F32),
            wts, prompt=True)

        ck = cache_k[l].transpose(0, 2, 3, 4, 1).reshape(-1, D_MODEL, PAGE_SIZE)
        cv = cache_v[l].reshape(-1, PAGE_SIZE * N_HEADS, V_DIM)
        mem_view = lambda c: c.reshape(bd, N_MEM, MEM_HEADS, MEM_HEAD_DIM // LANE, LANE).transpose(
            0, 1, 3, 2, 4).reshape(bd, N_MEM * MEM_ROW_GROUP, LANE)
        attn_s = lambda zb, k, v: _attn_sample(
            page_table, zb.reshape(bd, td, -1), k.reshape(bd, td, D_MODEL),
            v.reshape(bd, td, D_MODEL), ck, cv, *lams, gs, lam_init)
        xs, k_s, v_s, h_s, c_s, f_s = _layer(
            xs, attn_s, mem_view(cache_mem_k[l]), mem_view(cache_mem_v[l]),
            state_lru_h[l], state_lru_conv[l], state_ffn_conv[l], wts, prompt=False)

        k_p = k_p.reshape(bp, N_HEADS, 2, HEAD_DIM, tp).transpose(0, 4, 1, 2, 3)
        vals = (k_p, v_p.reshape(bp, tp, N_HEADS, V_DIM),
                mk_p.reshape(bp, N_MEM, MEM_HEADS, MEM_HEAD_DIM),
                mv_p.reshape(bp, N_MEM, MEM_HEADS, MEM_HEAD_DIM), h_p, c_p, f_p,
                k_s.reshape(bd, td, N_HEADS, 2, HEAD_DIM), v_s.reshape(bd, td, N_HEADS, V_DIM),
                h_s, c_s, f_s)
        for o, val in zip(outs, vals):
            o.append(val)

    return (xp, xs, *[jnp.stack(o) for o in outs])
```
